```python
import jax
import jax.numpy as jnp
from jax import lax
import numpy as np

D_MODEL = 2048
BATCH = 1
SEQ = 8192
DEPTH = 1

CHUNK = 64
N_HEADS = 16
HEAD_DIM = 128
KV_LATENT = 256
IDX_HEADS = 16
IDX_DIM = 64
TOPK_KEYS_MAX = 256
Q_BLOCK = 128
ATTN_SCALE = HEAD_DIM ** -0.5
LRU_WIDTH = 2048
LRU_BLOCKS = 16
LRU_BLOCK_DIM = LRU_WIDTH // LRU_BLOCKS
CONV_WIDTH = 4
RG_C = 8.0
N_EXPERTS = 64
TOP_K = 8
N_GROUPS = 8
TOPK_GROUPS = 4
EXPERT_FF = 512
SHARED_FF = 512
ROUTE_SCALE = 2.5
EXPERT_BLOCK = 128
LN_EPS = 1e-5
RMS_EPS = 1e-6

Q_DIM = N_HEADS * HEAD_DIM
IDX_Q_DIM = IDX_HEADS * IDX_DIM
PROJ_WIDTHS = (Q_DIM, KV_LATENT, IDX_Q_DIM, IDX_DIM, IDX_HEADS, LRU_WIDTH, LRU_WIDTH, D_MODEL, D_MODEL)
PROJ_DIM = sum(PROJ_WIDTHS)
PROJ_SPLITS = tuple(sum(PROJ_WIDTHS[:i + 1]) for i in range(len(PROJ_WIDTHS) - 1))

kernel_name = 'hybrid_dsa_rglru_moe_deepnorm'


def layer_norm(x, g, b):
    xf = x.astype(jnp.float32)
    mu = jnp.mean(xf, -1, keepdims=True)
    var = jnp.mean(jnp.square(xf - mu), -1, keepdims=True)
    y = (xf - mu) * lax.rsqrt(var + LN_EPS)
    return (y * g.astype(jnp.float32) + b.astype(jnp.float32)).astype(x.dtype)


def rms_norm(x, g):
    xf = x.astype(jnp.float32)
    y = xf * lax.rsqrt(jnp.mean(xf * xf, -1, keepdims=True) + RMS_EPS)
    return (y * g.astype(jnp.float32)).astype(x.dtype)


def swiglu(x, w_gate, w_up, w_down):
    return (jax.nn.silu(x @ w_gate) * (x @ w_up)) @ w_down


def dsa_attention(q, c, q_idx, k_idx, w_idx, w_uk, w_uv):
    B, T = q.shape[0], q.shape[1]
    n_sel = min(TOPK_KEYS_MAX, T // 4)
    n_blocks = T // Q_BLOCK
    q_lat = jnp.einsum('bthd,hdc->bthc', q, w_uk)
    key_pos = jnp.arange(T, dtype=jnp.int32)
    w_idx = w_idx * (IDX_HEADS ** -0.5)
    idx_scale = IDX_DIM ** -0.5

    def to_blocks(a):
        return jnp.moveaxis(a.reshape((B, n_blocks, Q_BLOCK) + a.shape[2:]), 1, 0)

    def query_block(args):
        qi, wi, ql, blk = args
        t = blk * Q_BLOCK + jnp.arange(Q_BLOCK, dtype=jnp.int32)
        limit = (t // CHUNK + 1) * CHUNK
        admissible = key_pos[None, :] < limit[:, None]
        logits = jnp.einsum('bqhd,bsd->bqhs', qi, k_idx) * idx_scale
        index_score = jnp.einsum('bqh,bqhs->bqs', wi, jax.nn.relu(logits)).astype(jnp.float32)
        index_score = jnp.where(admissible[None], index_score, -jnp.inf)
        _, sel = lax.top_k(index_score, n_sel)
        valid = sel < limit[None, :, None]
        c_sel = jax.vmap(lambda cb, ib: cb[ib])(c, sel)
        s = jnp.einsum('bqhc,bqkc->bqhk', ql, c_sel).astype(jnp.float32) * ATTN_SCALE
        s = jnp.where(valid[:, :, None, :], s, -jnp.inf)
        p = jax.nn.softmax(s, axis=-1).astype(c.dtype)
        return jnp.einsum('bqhk,bqkc->bqhc', p, c_sel)

    blocks = jnp.arange(n_blocks, dtype=jnp.int32)
    o_lat = lax.map(query_block, (to_blocks(q_idx), to_blocks(w_idx), to_blocks(q_lat), blocks))
    o_lat = jnp.moveaxis(o_lat, 0, 1).reshape(B, T, N_HEADS, KV_LATENT)
    o = jnp.einsum('bthc,hcd->bthd', o_lat, w_uv)
    return o.reshape(B, T, Q_DIM)


def rg_lru_branch(xr, yg, conv_w, conv_b, w_a, b_a, w_x, b_x, lam):
    B, T, C = xr.shape
    xc = lax.conv_general_dilated(
        xr, conv_w[:, None, :].astype(xr.dtype), window_strides=(1,),
        padding=[(CONV_WIDTH - 1, 0)], dimension_numbers=('NWC', 'WIO', 'NWC'),
        feature_group_count=C) + conv_b
    xb = xc.reshape(B, T, LRU_BLOCKS, LRU_BLOCK_DIM)
    r = jax.nn.sigmoid(jnp.einsum('btnd,nde->btne', xb, w_a).reshape(B, T, C) + b_a)
    i = jax.nn.sigmoid(jnp.einsum('btnd,nde->btne', xb, w_x).reshape(B, T, C) + b_x)
    log_a = -RG_C * r.astype(jnp.float32) * jax.nn.softplus(-lam.astype(jnp.float32))
    a = jnp.exp(log_a)
    u = jnp.sqrt(-jnp.expm1(2.0 * log_a)) * (i * xc).astype(jnp.float32)

    def combine(left, right):
        a1, b1 = left
        a2, b2 = right
        return a1 * a2, a2 * b1 + b2

    _, h = lax.associative_scan(combine, (a, u), axis=1)
    return h.astype(xr.dtype) * jax.nn.gelu(yg)


def routed_moe(h, w_router, router_bias, w_gate_e, w_up_e, w_down_e, w_gate_s, w_up_s, w_down_s):
    B, T, D = h.shape
    N = B * T
    hf = h.reshape(N, D)
    scores = jax.nn.sigmoid((hf @ w_router).astype(jnp.float32))
    biased = scores + router_bias.astype(jnp.float32)
    per_group = N_EXPERTS // N_GROUPS
    group_score = jnp.sum(lax.top_k(biased.reshape(N, N_GROUPS, per_group), 2)[0], -1)
    _, group_idx = lax.top_k(group_score, TOPK_GROUPS)
    group_mask = jnp.zeros((N, N_GROUPS), jnp.bool_).at[jnp.arange(N)[:, None], group_idx].set(True)
    expert_mask = jnp.repeat(group_mask, per_group, axis=1)
    _, expert_idx = lax.top_k(jnp.where(expert_mask, biased, -jnp.inf), TOP_K)
    gate = jnp.take_along_axis(scores, expert_idx, axis=1)
    gate = gate / jnp.sum(gate, -1, keepdims=True) * ROUTE_SCALE
    A = N * TOP_K
    e_flat = expert_idx.reshape(A)
    tok_flat = jnp.repeat(jnp.arange(N, dtype=jnp.int32), TOP_K)
    order = jnp.argsort(e_flat)
    e_sorted = e_flat[order]
    counts = jnp.bincount(e_flat, length=N_EXPERTS)
    padded = (counts + EXPERT_BLOCK - 1) // EXPERT_BLOCK * EXPERT_BLOCK
    pad_end = jnp.cumsum(padded)
    pad_start = pad_end - padded
    cnt_start = jnp.cumsum(counts) - counts
    dest = pad_start[e_sorted] + jnp.arange(A) - cnt_start[e_sorted]
    n_blk = -(-A // EXPERT_BLOCK) + N_EXPERTS
    rows = n_blk * EXPERT_BLOCK
    tok_buf = jnp.zeros((rows,), jnp.int32).at[dest].set(tok_flat[order])
    gate_buf = jnp.zeros((rows,), h.dtype).at[dest].set(gate.reshape(A)[order].astype(h.dtype))
    blk_expert = jnp.minimum(
        jnp.searchsorted(pad_end, jnp.arange(n_blk) * EXPERT_BLOCK, side='right'), N_EXPERTS - 1)

    def expert_rows(args):
        tok, g, e = args
        y = swiglu(hf[tok], w_gate_e[e], w_up_e[e], w_down_e[e])
        return y * g[:, None]

    ys = lax.map(expert_rows, (tok_buf.reshape(n_blk, EXPERT_BLOCK),
                               gate_buf.reshape(n_blk, EXPERT_BLOCK), blk_expert))
    routed = jnp.zeros_like(hf).at[tok_buf].add(ys.reshape(rows, D))
    shared = swiglu(hf, w_gate_s, w_up_s, w_down_s)
    return (routed + shared).reshape(B, T, D)


def setup_inputs(seed: int = 0) -> dict:
    key = jax.random.key(seed)
    ks = jax.random.split(key, 32)
    f32 = jnp.float32
    L = DEPTH
    beta = (8.0 * DEPTH) ** -0.25

    def normal(k, shape, scale):
        return jax.random.normal(k, shape, f32) * scale

    def gain(k, shape):
        return 1.0 + 0.02 * jax.random.normal(k, shape, f32)

    def bias(k, shape):
        return 0.02 * jax.random.normal(k, shape, f32)

    a_c = jax.random.uniform(ks[13], (L, LRU_WIDTH), f32, 0.9, 0.999)
    a_base = a_c ** (1.0 / RG_C)
    rg_lambda = jnp.log(a_base) - jnp.log1p(-a_base)
    return {
        'x': normal(ks[0], (BATCH, SEQ, D_MODEL), 1.0),
        'ln_in_g': gain(ks[1], (D_MODEL,)),
        'ln_in_b': bias(ks[2], (D_MODEL,)),
        'w_in': normal(ks[3], (L, D_MODEL, PROJ_DIM), D_MODEL ** -0.5),
        'kv_norm_g': gain(ks[4], (L, KV_LATENT)),
        'w_uk': normal(ks[5], (L, N_HEADS, HEAD_DIM, KV_LATENT), HEAD_DIM ** -0.5),
        'w_uv': normal(ks[6], (L, N_HEADS, KV_LATENT, HEAD_DIM), beta * KV_LATENT ** -0.5),
        'conv_w': normal(ks[7], (L, CONV_WIDTH, LRU_WIDTH), CONV_WIDTH ** -0.5),
        'conv_b': bias(ks[8], (L, LRU_WIDTH)),
        'w_rg_a': normal(ks[9], (L, LRU_BLOCKS, LRU_BLOCK_DIM, LRU_BLOCK_DIM), LRU_BLOCK_DIM ** -0.5),
        'b_rg_a': bias(ks[10], (L, LRU_WIDTH)),
        'w_rg_x': normal(ks[11], (L, LRU_BLOCKS, LRU_BLOCK_DIM, LRU_BLOCK_DIM), LRU_BLOCK_DIM ** -0.5),
        'b_rg_x': bias(ks[12], (L, LRU_WIDTH)),
        'rg_lambda': rg_lambda,
        'w_branch_a': normal(ks[14], (L, Q_DIM, D_MODEL), beta * Q_DIM ** -0.5),
        'w_branch_b': normal(ks[15], (L, LRU_WIDTH, D_MODEL), beta * LRU_WIDTH ** -0.5),
        'w_out': normal(ks[16], (L, D_MODEL, D_MODEL), beta * D_MODEL ** -0.5),
        'ln1_g': gain(ks[17], (L, D_MODEL)),
        'ln1_b': bias(ks[18], (L, D_MODEL)),
        'w_router': normal(ks[19], (L, D_MODEL, N_EXPERTS), D_MODEL ** -0.5),
        'router_bias': normal(ks[20], (L, N_EXPERTS), 0.01),
        'w_gate_e': normal(ks[21], (L, N_EXPERTS, D_MODEL, EXPERT_FF), D_MODEL ** -0.5),
        'w_up_e': normal(ks[22], (L, N_EXPERTS, D_MODEL, EXPERT_FF), beta * D_MODEL ** -0.5),
        'w_down_e': normal(ks[23], (L, N_EXPERTS, EXPERT_FF, D_MODEL), beta * EXPERT_FF ** -0.5),
        'w_gate_s': normal(ks[24], (L, D_MODEL, SHARED_FF), D_MODEL ** -0.5),
        'w_up_s': normal(ks[25], (L, D_MODEL, SHARED_FF), beta * D_MODEL ** -0.5),
        'w_down_s': normal(ks[26], (L, SHARED_FF, D_MODEL), beta * SHARED_FF ** -0.5),
        'ln2_g': gain(ks[27], (L, D_MODEL)),
        'ln2_b': bias(ks[28], (L, D_MODEL)),
    }


def reference(x, ln_in_g, ln_in_b, w_in, kv_norm_g, w_uk, w_uv, conv_w, conv_b, w_rg_a, b_rg_a,
              w_rg_x, b_rg_x, rg_lambda, w_branch_a, w_branch_b, w_out, ln1_g, ln1_b, w_router,
              router_bias, w_gate_e, w_up_e, w_down_e, w_gate_s, w_up_s, w_down_s, ln2_g, ln2_b):
    alpha = (2.0 * DEPTH) ** 0.25
    B, T, _ = x.shape
    h = layer_norm(x, ln_in_g, ln_in_b)
    for l in range(DEPTH):
        z = h @ w_in[l]
        q, c, qi, ki, wi, xr, yg, ga, gb = jnp.split(z, PROJ_SPLITS, axis=-1)
        c = rms_norm(c, kv_norm_g[l])
        attn = dsa_attention(q.reshape(B, T, N_HEADS, HEAD_DIM), c,
                             qi.reshape(B, T, IDX_HEADS, IDX_DIM), ki, wi, w_uk[l], w_uv[l])
        lru = rg_lru_branch(xr, yg, conv_w[l], conv_b[l], w_rg_a[l], b_rg_a[l],
                            w_rg_x[l], b_rg_x[l], rg_lambda[l])
        merged = (jax.nn.sigmoid(ga) * (attn @ w_branch_a[l])
                  + jax.nn.sigmoid(gb) * (lru @ w_branch_b[l]))
        h = layer_norm(alpha * h + merged @ w_out[l], ln1_g[l], ln1_b[l])
        ffn = routed_moe(h, w_router[l], router_bias[l], w_gate_e[l], w_up_e[l], w_down_e[l],
                         w_gate_s[l], w_up_s[l], w_down_s[l])
        h = layer_norm(alpha * h + ffn, ln2_g[l], ln2_b[l])
    return h
```

```python
import functools

import jax
import jax.numpy as jnp
from jax import lax
from jax.experimental import pallas as pl
from jax.experimental.pallas import tpu as pltpu

F32 = jnp.float32
BF16 = jnp.bfloat16
I32 = jnp.int32

D_MODEL = 2048
CHUNK = 64
N_HEADS = 16
HEAD_DIM = 128
KV_LATENT = 256
IDX_HEADS = 16
IDX_DIM = 64
TOPK_KEYS_MAX = 256
ATTN_SCALE = HEAD_DIM ** -0.5
LRU_WIDTH = 2048
LRU_BLOCKS = 16
LRU_BLOCK_DIM = LRU_WIDTH // LRU_BLOCKS
CONV_WIDTH = 4
RG_C = 8.0
N_EXPERTS = 64
TOP_K = 8
N_GROUPS = 8
TOPK_GROUPS = 4
EXPERT_FF = 512
SHARED_FF = 512
ROUTE_SCALE = 2.5
LN_EPS = 1e-5
RMS_EPS = 1e-6
Q_DIM = N_HEADS * HEAD_DIM
IDX_Q_DIM = IDX_HEADS * IDX_DIM
DEPTH = 1
ALPHA = (2.0 * DEPTH) ** 0.25

VMEM_LIMIT_V7X = 56 * 1024 * 1024
LANES = 128

ATTN_QB = 128
ATTN_TK = 512
EXPERT_TILE = 128
ROW_TILE = 128

NT_DIMS = (((1,), (1,)), ((), ()))


def _cparams(n_axes=1, vmem=VMEM_LIMIT_V7X):
    return pltpu.CompilerParams(dimension_semantics=("arbitrary",) * n_axes, vmem_limit_bytes=vmem)


def _layer_norm_rows(x, g, b):
    mu = jnp.mean(x, axis=-1, keepdims=True)
    xc = x - mu
    var = jnp.mean(xc * xc, axis=-1, keepdims=True)
    return xc * lax.rsqrt(var + LN_EPS) * g + b


def _sigmoid(x):
    return 1.0 / (1.0 + jnp.exp(-x))


def _ln_in_kernel(x_ref, g_ref, b_ref, hf_ref, hb_ref):
    y = _layer_norm_rows(x_ref[...], g_ref[...], b_ref[...])
    hf_ref[...] = y
    hb_ref[...] = y.astype(BF16)


def ln_in(x, g, b, tm=512):
    T, D = x.shape
    row = pl.BlockSpec((tm, D), lambda i: (i, 0))
    vec = pl.BlockSpec((1, D), lambda i: (0, 0))
    return pl.pallas_call(
        _ln_in_kernel, grid=(T // tm,), in_specs=[row, vec, vec], out_specs=[row, row],
        out_shape=[jax.ShapeDtypeStruct((T, D), F32), jax.ShapeDtypeStruct((T, D), BF16)],
        compiler_params=_cparams(), name="ln_in")(x, g.reshape(1, D), b.reshape(1, D))


def _mm_kernel(x_ref, w_ref, o_ref):
    o_ref[...] = jnp.dot(x_ref[...], w_ref[...], preferred_element_type=F32).astype(o_ref.dtype)


def matmul(x, w, out_dtype, tm, tn, name):
    M, K = x.shape
    N = w.shape[1]
    return pl.pallas_call(
        _mm_kernel, grid=(M // tm, N // tn),
        in_specs=[pl.BlockSpec((tm, K), lambda i, j: (i, 0)), pl.BlockSpec((K, tn), lambda i, j: (0, j))],
        out_specs=pl.BlockSpec((tm, tn), lambda i, j: (i, j)),
        out_shape=jax.ShapeDtypeStruct((M, N), out_dtype),
        compiler_params=_cparams(2), name=name)(x, w)


def _mm_rms_kernel(x_ref, w_ref, g_ref, o_ref):
    c = jnp.dot(x_ref[...], w_ref[...], preferred_element_type=F32)
    y = c * lax.rsqrt(jnp.mean(c * c, axis=-1, keepdims=True) + RMS_EPS) * g_ref[...]
    o_ref[...] = y.astype(o_ref.dtype)


def matmul_rms(x, w, g, tm=1024):
    M, K = x.shape
    N = w.shape[1]
    return pl.pallas_call(
        _mm_rms_kernel, grid=(M // tm,),
        in_specs=[pl.BlockSpec((tm, K), lambda i: (i, 0)), pl.BlockSpec((K, N), lambda i: (0, 0)),
                  pl.BlockSpec((1, N), lambda i: (0, 0))],
        out_specs=pl.BlockSpec((tm, N), lambda i: (i, 0)),
        out_shape=jax.ShapeDtypeStruct((M, N), BF16),
        compiler_params=_cparams(), name="proj_kv_rms")(x, w, g.reshape(1, N))


def _attn_kernel(q_ref, qi_ref, kw_ref, kidx_ref, c_ref, wuk_ref, wuv_ref, o_ref,
                 sc_ref, big_ref, p_ref, acc_ref, m_ref, l_ref, al_ref, ql_ref, qis_ref, thr_ref, j_ref,
                 *, QB, TK, NSEL):
    H = N_HEADS
    b = pl.program_id(0)
    lmax = (b + 1) * QB
    nk = lax.div(lmax + (TK - 1), TK)
    ksel = float(NSEL)
    neg_inf = float("-inf")
    fmax = float(jnp.finfo(jnp.float32).max)

    row = lax.broadcasted_iota(I32, (QB, 1), 0)
    tpos = b * QB + row
    limit = (lax.shift_right_logical(tpos, 6) + 1) * CHUNK
    limit_f = limit.astype(F32)

    for h in range(H):
        rows = slice(h * QB, (h + 1) * QB)
        qis_ref[rows, :] = qi_ref[:, h * IDX_DIM:(h + 1) * IDX_DIM]
        qlat = jnp.dot(q_ref[:, h * HEAD_DIM:(h + 1) * HEAD_DIM], wuk_ref[h], preferred_element_type=F32)
        ql_ref[rows, :] = (qlat * ATTN_SCALE).astype(BF16)
    wscale = (IDX_HEADS ** -0.5) * (IDX_DIM ** -0.5)
    wcols = [kw_ref[:, IDX_DIM + h:IDX_DIM + h + 1] * wscale for h in range(H)]

    def p1(kt, carry):
        k0 = pl.multiple_of(kt * TK, TK)
        big_ref[...] = lax.dot_general(qis_ref[...], kidx_ref[pl.ds(k0, TK), :], NT_DIMS,
                                       preferred_element_type=F32)
        for j in range(TK // LANES):
            cols = slice(j * LANES, (j + 1) * LANES)
            acc = jnp.zeros((QB, LANES), F32)
            for h in range(H):
                acc = acc + wcols[h] * jnp.maximum(big_ref[h * QB:(h + 1) * QB, cols], 0.0)
            col = k0 + j * LANES + lax.broadcasted_iota(I32, (QB, LANES), 1)
            sc_ref[kt, :, cols] = jnp.where(col < limit, acc, neg_inf)
        return carry

    lax.fori_loop(0, nk, p1, 0)

    def count_ge(t):
        tb = jnp.broadcast_to(t, (QB, LANES))

        def body(kt, cnt):
            for j in range(TK // LANES):
                cnt = cnt + jnp.where(sc_ref[kt, :, j * LANES:(j + 1) * LANES] >= tb, 1.0, 0.0)
            return cnt

        cnt = lax.fori_loop(0, nk, body, jnp.zeros((QB, LANES), F32))
        return jnp.sum(cnt, axis=1, keepdims=True)

    def minmax_body(kt, carry):
        mn, mx = carry
        for j in range(TK // LANES):
            x = sc_ref[kt, :, j * LANES:(j + 1) * LANES]
            mx = jnp.maximum(mx, x)
            mn = jnp.minimum(mn, jnp.where(x == neg_inf, fmax, x))
        return mn, mx

    mn, mx = lax.fori_loop(0, nk, minmax_body,
                           (jnp.full((QB, LANES), fmax, F32), jnp.full((QB, LANES), neg_inf, F32)))
    mn = jnp.min(mn, axis=1, keepdims=True)
    mx = jnp.max(mx, axis=1, keepdims=True)

    c_top = count_ge(mx)
    one = jnp.ones((QB, 1), F32)
    zero = jnp.zeros((QB, 1), F32)
    all_sel = jnp.where(limit_f <= ksel, one, zero)
    top_tie = (1.0 - all_sel) * jnp.where(c_top >= ksel, one, zero)
    lo0 = jnp.where(top_tie > 0, mx, mn)
    c_lo0 = jnp.where(top_tie > 0, c_top, limit_f)
    c_hi0 = jnp.where(top_tie > 0, zero, c_top)
    thr0 = jnp.where(all_sel > 0, -fmax, mx)
    tie0 = top_tie * jnp.where(c_top > ksel, one, zero)
    done0 = jnp.maximum(all_sel, top_tie)

    def bis_cond(st):
        return jnp.logical_and(st[0] > 0.5, st[1] < 400)

    def bis_body(st):
        _, it, lo, hi, c_lo, c_hi, thr, tie, done = st
        act = done < 0.5
        mid = 0.5 * lo + 0.5 * hi
        adjacent = jnp.logical_or(mid <= lo, mid >= hi)
        c = count_ge(mid)
        ge = c >= ksel
        exact = c == ksel
        fin_tie = jnp.logical_and(adjacent, jnp.logical_not(exact))
        finished = jnp.logical_and(act, jnp.logical_or(exact, adjacent))
        thr = jnp.where(jnp.logical_and(act, exact), mid, jnp.where(jnp.logical_and(act, fin_tie), lo, thr))
        tie = jnp.where(jnp.logical_and(act, fin_tie), one, tie)
        upd = jnp.logical_and(act, jnp.logical_not(jnp.logical_or(exact, adjacent)))
        up_lo = jnp.logical_and(upd, ge)
        up_hi = jnp.logical_and(upd, jnp.logical_not(ge))
        lo = jnp.where(up_lo, mid, lo)
        c_lo = jnp.where(up_lo, c, c_lo)
        hi = jnp.where(up_hi, mid, hi)
        c_hi = jnp.where(up_hi, c, c_hi)
        done = jnp.where(finished, one, done)
        n_act = jnp.max(1.0 - done)
        return n_act, it + 1, lo, hi, c_lo, c_hi, thr, tie, done

    st = lax.while_loop(bis_cond, bis_body,
                        (jnp.max(1.0 - done0), jnp.int32(0), lo0, mx, c_lo0, c_hi0, thr0, tie0, done0))
    _, _, _, _, _, c_hi, thr, tie, _ = st
    thr_ref[...] = thr
    big_j = float(2 ** 30)
    j_ref[...] = jnp.full((QB, 1), big_j, F32)

    @pl.when(jnp.max(tie) > 0.5)
    def _():
        need = ksel - c_hi
        thr_b = jnp.broadcast_to(thr, (QB, LANES))

        def count_tie_below(jb):
            jbb = jnp.broadcast_to(jb, (QB, LANES))

            def body(kt, cnt):
                for j in range(TK // LANES):
                    x = sc_ref[kt, :, j * LANES:(j + 1) * LANES]
                    colf = (kt * TK + j * LANES + lax.broadcasted_iota(I32, (QB, LANES), 1)).astype(F32)
                    hit = jnp.logical_and(x == thr_b, colf < jbb)
                    cnt = cnt + jnp.where(hit, 1.0, 0.0)
                return cnt

            cnt = lax.fori_loop(0, nk, body, jnp.zeros((QB, LANES), F32))
            return jnp.sum(cnt, axis=1, keepdims=True)

        def jbody(_, carry):
            jlo, jhi = carry
            jm = jnp.floor((jlo + jhi) * 0.5)
            ok = count_tie_below(jm) >= need
            return jnp.where(ok, jlo, jm), jnp.where(ok, jm, jhi)

        n_steps = max(1, int(sc_ref.shape[0] * TK).bit_length())
        _, jhi = lax.fori_loop(0, n_steps, jbody, (zero, zero + (nk * TK).astype(F32)))
        j_ref[...] = jnp.where(tie > 0.5, jhi, big_j)

    m_ref[...] = jnp.full(m_ref.shape, -1e30, F32)
    l_ref[...] = jnp.zeros(l_ref.shape, F32)
    acc_ref[...] = jnp.zeros(acc_ref.shape, F32)

    def p3(kt, carry):
        k0 = pl.multiple_of(kt * TK, TK)
        ct = c_ref[pl.ds(k0, TK), :]
        big_ref[...] = lax.dot_general(ql_ref[...], ct, NT_DIMS, preferred_element_type=F32)
        x = sc_ref[kt]
        colf = (k0 + lax.broadcasted_iota(I32, (QB, TK), 1)).astype(F32)
        t = thr_ref[...]
        sel = jnp.logical_or(x > t, jnp.logical_and(x == t, colf < j_ref[...]))
        for h in range(H):
            rows = slice(h * QB, (h + 1) * QB)
            s = jnp.where(sel, big_ref[rows, :], neg_inf)
            m_old = m_ref[rows, :]
            m_new = jnp.maximum(m_old, jnp.max(s, axis=1, keepdims=True))
            p = jnp.exp(s - m_new)
            alpha = jnp.exp(m_old - m_new)
            l_ref[rows, :] = alpha * l_ref[rows, :] + jnp.sum(p, axis=1, keepdims=True)
            m_ref[rows, :] = m_new
            al_ref[rows, :] = alpha
            p_ref[rows, :] = p.astype(BF16)
        pv = jnp.dot(p_ref[...], ct, preferred_element_type=F32)
        acc_ref[...] = al_ref[...] * acc_ref[...] + pv
        return carry

    lax.fori_loop(0, nk, p3, 0)

    for h in range(H):
        rows = slice(h * QB, (h + 1) * QB)
        o_lat = (acc_ref[rows, :] / l_ref[rows, :]).astype(BF16)
        o_ref[:, h * HEAD_DIM:(h + 1) * HEAD_DIM] = jnp.dot(
            o_lat, wuv_ref[h], preferred_element_type=F32).astype(o_ref.dtype)


def dsa_attention(qq, kw, c, w_uk, w_uv, qb=ATTN_QB, tk=ATTN_TK):
    T = c.shape[0]
    tk = min(tk, T)
    nsel = min(TOPK_KEYS_MAX, T // 4)
    assert qb % CHUNK == 0 and T % tk == 0 and T % qb == 0 and tk % qb == 0
    kidx = kw[:, :IDX_DIM].astype(BF16)
    H = N_HEADS
    kernel = functools.partial(_attn_kernel, QB=qb, TK=tk, NSEL=nsel)
    full = lambda shape: pl.BlockSpec(shape, lambda i: (0,) * len(shape))
    return pl.pallas_call(
        kernel, grid=(T // qb,),
        in_specs=[pl.BlockSpec((qb, Q_DIM), lambda i: (i, 0)),
                  pl.BlockSpec((qb, IDX_Q_DIM), lambda i: (i, Q_DIM // IDX_Q_DIM)),
                  pl.BlockSpec((qb, LANES), lambda i: (i, 0)),
                  full((T, IDX_DIM)), full((T, KV_LATENT)),
                  full((H, HEAD_DIM, KV_LATENT)), full((H, KV_LATENT, HEAD_DIM))],
        out_specs=pl.BlockSpec((qb, Q_DIM), lambda i: (i, 0)),
        out_shape=jax.ShapeDtypeStruct((T, Q_DIM), BF16),
        scratch_shapes=[pltpu.VMEM((T // tk, qb, tk), F32),
                        pltpu.VMEM((H * qb, tk), F32),
                        pltpu.VMEM((H * qb, tk), BF16),
                        pltpu.VMEM((H * qb, KV_LATENT), F32),
                        pltpu.VMEM((H * qb, 1), F32), pltpu.VMEM((H * qb, 1), F32), pltpu.VMEM((H * qb, 1), F32),
                        pltpu.VMEM((H * qb, KV_LATENT), BF16),
                        pltpu.VMEM((H * qb, IDX_DIM), BF16),
                        pltpu.VMEM((qb, 1), F32), pltpu.VMEM((qb, 1), F32)],
        compiler_params=_cparams(), name="dsa_attention")(
            qq, qq, kw, kidx, c, w_uk.astype(BF16), w_uv.astype(BF16))


def _gelu_tanh(x):
    return 0.5 * x * (1.0 + jnp.tanh(0.7978845608028654 * (x + 0.044715 * (x * x * x))))


def _lru_kernel(xr_ref, yg_ref, cw_ref, cb_ref, wa_ref, ba_ref, wx_ref, bx_ref, lam_ref, o_ref,
                ext_ref, a_ref, u_ref, hs_ref, h_ref, *, TT):
    i = pl.program_id(0)
    C = LRU_WIDTH

    @pl.when(i == 0)
    def _():
        ext_ref[0:8, :] = jnp.zeros((8, C), F32)
        h_ref[...] = jnp.zeros((1, C), F32)

    ext_ref[8:8 + TT, :] = xr_ref[...]
    xc = jnp.zeros((TT, C), F32) + cb_ref[...]
    for j in range(CONV_WIDTH):
        off = 8 - (CONV_WIDTH - 1) + j
        xc = xc + cw_ref[j:j + 1, :] * ext_ref[off:off + TT, :]
    ext_ref[0:8, :] = ext_ref[TT:TT + 8, :]

    lam = lam_ref[...]
    softplus_neg = jnp.maximum(-lam, 0.0) + jnp.log1p(jnp.exp(-jnp.abs(lam)))
    xcb = xc.astype(BF16)
    for n in range(LRU_BLOCKS):
        cols = slice(n * LRU_BLOCK_DIM, (n + 1) * LRU_BLOCK_DIM)
        xb = xcb[:, cols]
        r = _sigmoid(jnp.dot(xb, wa_ref[n], preferred_element_type=F32) + ba_ref[:, cols])
        g = _sigmoid(jnp.dot(xb, wx_ref[n], preferred_element_type=F32) + bx_ref[:, cols])
        log_a = (-RG_C) * r * softplus_neg[:, cols]
        a_ref[:, cols] = jnp.exp(log_a)
        th = jnp.tanh(log_a)
        u_ref[:, cols] = jnp.sqrt((-2.0 * th) / (1.0 - th)) * (g * xc[:, cols])

    def step(t, h):
        h = a_ref[pl.ds(t, 1), :] * h + u_ref[pl.ds(t, 1), :]
        hs_ref[pl.ds(t, 1), :] = h
        return h

    h_ref[...] = lax.fori_loop(0, TT, step, h_ref[...], unroll=8)
    o_ref[...] = (hs_ref[...] * _gelu_tanh(yg_ref[...])).astype(o_ref.dtype)


def rg_lru(zz, conv_w, conv_b, w_a, b_a, w_x, b_x, lam, tt=256):
    T = zz.shape[0]
    tt = min(tt, T)
    C = LRU_WIDTH
    vec = pl.BlockSpec((1, C), lambda i: (0, 0))
    blk = pl.BlockSpec((LRU_BLOCKS, LRU_BLOCK_DIM, LRU_BLOCK_DIM), lambda i: (0, 0, 0))
    return pl.pallas_call(
        functools.partial(_lru_kernel, TT=tt), grid=(T // tt,),
        in_specs=[pl.BlockSpec((tt, C), lambda i: (i, 0)), pl.BlockSpec((tt, C), lambda i: (i, 1)),
                  pl.BlockSpec((CONV_WIDTH, C), lambda i: (0, 0)), vec, blk, vec, blk, vec, vec],
        out_specs=pl.BlockSpec((tt, C), lambda i: (i, 0)),
        out_shape=jax.ShapeDtypeStruct((T, C), BF16),
        scratch_shapes=[pltpu.VMEM((tt + 8, C), F32), pltpu.VMEM((tt, C), F32), pltpu.VMEM((tt, C), F32),
                        pltpu.VMEM((tt, C), F32), pltpu.VMEM((1, C), F32)],
        compiler_params=_cparams(), name="rg_lru")(
            zz, zz, conv_w, conv_b.reshape(1, C), w_a.astype(BF16), b_a.reshape(1, C),
            w_x.astype(BF16), b_x.reshape(1, C), lam.reshape(1, C))


def _merge_kernel(a_ref, l_ref, wa_ref, wb_ref, ga_ref, gb_ref, o_ref):
    ya = jnp.dot(a_ref[...], wa_ref[...], preferred_element_type=F32)
    yb = jnp.dot(l_ref[...], wb_ref[...], preferred_element_type=F32)
    o_ref[...] = (_sigmoid(ga_ref[...]) * ya + _sigmoid(gb_ref[...]) * yb).astype(o_ref.dtype)


def merge_branches(attn, lru, w_a, w_b, zz, tm=512, tn=512):
    T, D = attn.shape
    nb = D // tn
    return pl.pallas_call(
        _merge_kernel, grid=(T // tm, nb),
        in_specs=[pl.BlockSpec((tm, D), lambda i, j: (i, 0)), pl.BlockSpec((tm, D), lambda i, j: (i, 0)),
                  pl.BlockSpec((D, tn), lambda i, j: (0, j)), pl.BlockSpec((D, tn), lambda i, j: (0, j)),
                  pl.BlockSpec((tm, tn), lambda i, j: (i, 2 * nb + j)),
                  pl.BlockSpec((tm, tn), lambda i, j: (i, 3 * nb + j))],
        out_specs=pl.BlockSpec((tm, tn), lambda i, j: (i, j)),
        out_shape=jax.ShapeDtypeStruct((T, D), BF16),
        compiler_params=_cparams(2), name="merge_branches")(attn, lru, w_a, w_b, zz, zz)


def _post1_kernel(m_ref, w_ref, h_ref, g_ref, b_ref, o_ref):
    y = ALPHA * h_ref[...] + jnp.dot(m_ref[...], w_ref[...], preferred_element_type=F32)
    o_ref[...] = _layer_norm_rows(y, g_ref[...], b_ref[...])


def out_proj_ln(merged, w_out, h0, g, b, tm=512):
    T, D = merged.shape
    row = lambda dt: pl.BlockSpec((tm, D), lambda i: (i, 0))
    vec = pl.BlockSpec((1, D), lambda i: (0, 0))
    return pl.pallas_call(
        _post1_kernel, grid=(T // tm,),
        in_specs=[row(BF16), pl.BlockSpec((D, D), lambda i: (0, 0)), row(F32), vec, vec],
        out_specs=row(F32), out_shape=jax.ShapeDtypeStruct((T, D), F32),
        compiler_params=_cparams(), name="out_proj_ln1")(merged, w_out, h0, g.reshape(1, D), b.reshape(1, D))


def _router_kernel(h_ref, wr_ref, rb_ref, eidx_ref, gate_ref, pos_ref, cnt_ref, carry_ref, *, TM):
    i = pl.program_id(0)
    E = N_EXPERTS
    per_group = E // N_GROUPS
    neg_inf = float("-inf")

    @pl.when(i == 0)
    def _():
        carry_ref[...] = jnp.zeros((1, E), F32)

    h = h_ref[...]
    hh = h.astype(BF16)
    hl = (h - hh.astype(F32)).astype(BF16)
    w = wr_ref[...]
    wh = w.astype(BF16)
    wl = (w - wh.astype(F32)).astype(BF16)
    logits = (jnp.dot(hh, wh, preferred_element_type=F32)
              + (jnp.dot(hh, wl, preferred_element_type=F32) + jnp.dot(hl, wh, preferred_element_type=F32)))
    scores = _sigmoid(logits)
    biased = scores + rb_ref[...]
    lane = lax.broadcasted_iota(I32, (TM, E), 1)

    def first_argmax(v):
        m = jnp.max(v, axis=1, keepdims=True)
        idx = jnp.min(jnp.where(v == m, lane, E), axis=1, keepdims=True)
        return m, idx

    gscore = []
    for g in range(N_GROUPS):
        in_g = jnp.logical_and(lane >= g * per_group, lane < (g + 1) * per_group)
        vg = jnp.where(in_g, biased, neg_inf)
        m1, i1 = first_argmax(vg)
        m2 = jnp.max(jnp.where(lane == i1, neg_inf, vg), axis=1, keepdims=True)
        gscore.append(m1 + m2)
    cand = jnp.full((TM, E), neg_inf, F32)
    for g in range(N_GROUPS):
        rank = jnp.zeros((TM, 1), F32)
        for g2 in range(N_GROUPS):
            if g2 == g:
                continue
            ahead = (gscore[g2] >= gscore[g]) if g2 < g else (gscore[g2] > gscore[g])
            rank = rank + jnp.where(ahead, 1.0, 0.0)
        in_g = jnp.logical_and(lane >= g * per_group, lane < (g + 1) * per_group)
        keep = jnp.logical_and(in_g, rank < float(TOPK_GROUPS))
        cand = jnp.where(keep, biased, cand)

    lane_k = lax.broadcasted_iota(I32, (TM, TOP_K), 1)
    sel = jnp.zeros((TM, E), F32)
    onehots, gates = [], []
    eidx = jnp.zeros((TM, TOP_K), I32)
    for k in range(TOP_K):
        _, ik = first_argmax(cand)
        oh = lane == ik
        onehots.append(oh)
        gates.append(jnp.sum(jnp.where(oh, scores, 0.0), axis=1, keepdims=True))
        cand = jnp.where(oh, neg_inf, cand)
        sel = sel + jnp.where(oh, 1.0, 0.0)
        eidx = jnp.where(lane_k == k, ik, eidx)
    gsum = gates[0]
    for k in range(1, TOP_K):
        gsum = gsum + gates[k]

    r_i = lax.broadcasted_iota(I32, (TM, TM), 0)
    c_i = lax.broadcasted_iota(I32, (TM, TM), 1)
    tri = jnp.where(c_i < r_i, 1.0, 0.0).astype(BF16)
    prefix = jnp.dot(tri, sel.astype(BF16), preferred_element_type=F32) + carry_ref[...]
    gate = jnp.zeros((TM, TOP_K), F32)
    pos = jnp.zeros((TM, TOP_K), F32)
    for k in range(TOP_K):
        gk = gates[k] / gsum * ROUTE_SCALE
        pk = jnp.sum(jnp.where(onehots[k], prefix, 0.0), axis=1, keepdims=True)
        gate = jnp.where(lane_k == k, gk, gate)
        pos = jnp.where(lane_k == k, pk, pos)
    eidx_ref[...] = eidx
    gate_ref[...] = gate
    pos_ref[...] = pos.astype(I32)
    carry_ref[...] = carry_ref[...] + jnp.sum(sel, axis=0, keepdims=True)
    cnt_ref[...] = carry_ref[...]


def router(h1, w_router, router_bias, tm=512):
    T, D = h1.shape
    tm = min(tm, T)
    E = N_EXPERTS
    out8 = pl.BlockSpec((tm, TOP_K), lambda i: (i, 0))
    return pl.pallas_call(
        functools.partial(_router_kernel, TM=tm), grid=(T // tm,),
        in_specs=[pl.BlockSpec((tm, D), lambda i: (i, 0)), pl.BlockSpec((D, E), lambda i: (0, 0)),
                  pl.BlockSpec((1, E), lambda i: (0, 0))],
        out_specs=[out8, out8, out8, pl.BlockSpec((1, E), lambda i: (0, 0))],
        out_shape=[jax.ShapeDtypeStruct((T, TOP_K), I32), jax.ShapeDtypeStruct((T, TOP_K), F32),
                   jax.ShapeDtypeStruct((T, TOP_K), I32), jax.ShapeDtypeStruct((1, E), F32)],
        scratch_shapes=[pltpu.VMEM((1, E), F32)],
        compiler_params=_cparams(), name="router")(h1, w_router, router_bias.reshape(1, E))


def _row_copy(src_hbm, src_row, dst, dst_row, sem):
    return pltpu.make_async_copy(src_hbm.at[pl.ds(src_row, 1)], dst.at[pl.ds(dst_row, 1)], sem)


def _dispatch_kernel(dest_ref, h_hbm, xs_hbm, sem, *, TM):
    base = pl.program_id(0) * TM

    def start(n, carry):
        for k in range(TOP_K):
            _row_copy(h_hbm, base + n, xs_hbm, dest_ref[n * TOP_K + k], sem).start()
        return carry

    def wait(n, carry):
        for k in range(TOP_K):
            _row_copy(h_hbm, 0, xs_hbm, 0, sem).wait()
        return carry

    lax.fori_loop(0, TM, start, 0)
    lax.fori_loop(0, TM, wait, 0)


def dispatch(dest, h1, tm=ROW_TILE):
    T, D = h1.shape
    return pl.pallas_call(
        functools.partial(_dispatch_kernel, TM=tm), grid=(T // tm,),
        in_specs=[pl.BlockSpec((tm * TOP_K,), lambda i: (i,), memory_space=pltpu.SMEM),
                  pl.BlockSpec(memory_space=pl.ANY)],
        out_specs=pl.BlockSpec(memory_space=pl.ANY),
        out_shape=jax.ShapeDtypeStruct((T * TOP_K, D), h1.dtype),
        scratch_shapes=[pltpu.SemaphoreType.DMA(())],
        compiler_params=_cparams(), name="moe_dispatch")(dest, h1)


def _expert_kernel(vt_ref, ve_ref, vlo_ref, vhi_ref, vfl_ref, x_ref, wg_ref, wu_ref, wd_ref, o_ref,
                   wgb_ref, wub_ref, wdb_ref, *, TE):
    v = pl.program_id(0)
    flags = vfl_ref[v]

    @pl.when((flags & 2) != 0)
    def _():
        wgb_ref[...] = wg_ref[0].astype(BF16)
        wub_ref[...] = wu_ref[0].astype(BF16)
        wdb_ref[...] = wd_ref[0].astype(BF16)

    @pl.when((flags & 4) != 0)
    def _():
        x = x_ref[...].astype(BF16)
        g = jnp.dot(x, wgb_ref[...], preferred_element_type=F32)
        u = jnp.dot(x, wub_ref[...], preferred_element_type=F32)
        rows = vt_ref[v] * TE + lax.broadcasted_iota(I32, (TE, 1), 0)
        mine = jnp.logical_and(rows >= vlo_ref[v], rows < vhi_ref[v])
        mid = jnp.where(mine, (g * _sigmoid(g)) * u, 0.0).astype(BF16)
        y = jnp.dot(mid, wdb_ref[...], preferred_element_type=F32)

        @pl.when((flags & 1) != 0)
        def _():
            o_ref[...] = y

        @pl.when((flags & 1) == 0)
        def _():
            o_ref[...] = o_ref[...] + y


def expert_ffn(tables, xs, w_gate_e, w_up_e, w_down_e, te=EXPERT_TILE):
    R, D = xs.shape
    F = EXPERT_FF
    nv = tables[0].shape[0]
    grid_spec = pltpu.PrefetchScalarGridSpec(
        num_scalar_prefetch=5, grid=(nv,),
        in_specs=[pl.BlockSpec((te, D), lambda v, vt, ve, lo, hi, fl: (vt[v], 0)),
                  pl.BlockSpec((1, D, F), lambda v, vt, ve, lo, hi, fl: (ve[v], 0, 0)),
                  pl.BlockSpec((1, D, F), lambda v, vt, ve, lo, hi, fl: (ve[v], 0, 0)),
                  pl.BlockSpec((1, F, D), lambda v, vt, ve, lo, hi, fl: (ve[v], 0, 0))],
        out_specs=pl.BlockSpec((te, D), lambda v, vt, ve, lo, hi, fl: (vt[v], 0)),
        scratch_shapes=[pltpu.VMEM((D, F), BF16), pltpu.VMEM((D, F), BF16), pltpu.VMEM((F, D), BF16)])
    return pl.pallas_call(
        functools.partial(_expert_kernel, TE=te), grid_spec=grid_spec,
        out_shape=jax.ShapeDtypeStruct((R, D), F32),
        compiler_params=_cparams(), name="expert_ffn")(*tables, xs, w_gate_e, w_up_e, w_down_e)


def visit_tables(counts, n_rows, te=EXPERT_TILE):
    E = counts.shape[0]
    n_tiles = n_rows // te
    nv = n_tiles + E
    ends = jnp.cumsum(counts)
    starts = ends - counts
    first_tile = starts // te
    last_tile = jnp.maximum(ends - 1, 0) // te
    nvis = jnp.where(counts > 0, last_tile - first_tile + 1, 0)
    vis_end = jnp.cumsum(nvis)
    vis_start = vis_end - nvis
    total = vis_end[-1]
    v = jnp.arange(nv, dtype=I32)
    vc = jnp.minimum(v, total - 1)
    e = jnp.searchsorted(vis_end, vc, side="right").astype(I32)
    tile = (first_tile[e] + vc - vis_start[e]).astype(I32)
    lo = jnp.maximum(starts[e], tile * te).astype(I32)
    hi = jnp.minimum(ends[e], (tile + 1) * te).astype(I32)
    active = v < total
    prev_tile = jnp.concatenate([jnp.full((1,), -1, I32), tile[:-1]])
    prev_e = jnp.concatenate([jnp.full((1,), -1, I32), e[:-1]])
    flags = jnp.where(active, 4 + jnp.where(tile != prev_tile, 1, 0) + jnp.where(e != prev_e, 2, 0), 0).astype(I32)
    return tile, e, lo, hi, flags


def _combine_kernel(dest_ref, gate_ref, h_ref, wgs_ref, wus_ref, wds_ref, g_ref, b_ref, ys_hbm, o_ref,
                    buf_ref, sem, *, TM):
    def start(n, carry):
        for k in range(TOP_K):
            _row_copy(ys_hbm, dest_ref[n * TOP_K + k], buf_ref.at[k], n, sem).start()
        return carry

    def wait(n, carry):
        for k in range(TOP_K):
            _row_copy(ys_hbm, 0, buf_ref.at[k], 0, sem).wait()
        return carry

    lax.fori_loop(0, TM, start, 0)
    h = h_ref[...]
    hb = h.astype(BF16)
    g = jnp.dot(hb, wgs_ref[...], preferred_element_type=F32)
    u = jnp.dot(hb, wus_ref[...], preferred_element_type=F32)
    shared = jnp.dot(((g * _sigmoid(g)) * u).astype(BF16), wds_ref[...], preferred_element_type=F32)
    lax.fori_loop(0, TM, wait, 0)
    routed = jnp.zeros(h.shape, F32)
    for k in range(TOP_K):
        routed = routed + buf_ref[k] * gate_ref[:, k:k + 1]
    y = ALPHA * h + (routed + shared)
    o_ref[...] = _layer_norm_rows(y, g_ref[...], b_ref[...])


def combine(dest, gate, h1, w_gate_s, w_up_s, w_down_s, g, b, ys, tm=ROW_TILE):
    T, D = h1.shape
    F = SHARED_FF
    row = pl.BlockSpec((tm, D), lambda i: (i, 0))
    vec = pl.BlockSpec((1, D), lambda i: (0, 0))
    return pl.pallas_call(
        functools.partial(_combine_kernel, TM=tm), grid=(T // tm,),
        in_specs=[pl.BlockSpec((tm * TOP_K,), lambda i: (i,), memory_space=pltpu.SMEM),
                  pl.BlockSpec((tm, TOP_K), lambda i: (i, 0)), row,
                  pl.BlockSpec((D, F), lambda i: (0, 0)), pl.BlockSpec((D, F), lambda i: (0, 0)),
                  pl.BlockSpec((F, D), lambda i: (0, 0)), vec, vec,
                  pl.BlockSpec(memory_space=pl.ANY)],
        out_specs=row, out_shape=jax.ShapeDtypeStruct((T, D), F32),
        scratch_shapes=[pltpu.VMEM((TOP_K, tm, D), F32), pltpu.SemaphoreType.DMA(())],
        compiler_params=_cparams(), name="moe_combine_ln2")(
            dest, gate, h1, w_gate_s.astype(BF16), w_up_s.astype(BF16), w_down_s.astype(BF16),
            g.reshape(1, D), b.reshape(1, D), ys)


def _mixer(h0f, h0b, w_in, kv_norm_g, w_uk, w_uv, conv_w, conv_b, w_rg_a, b_rg_a, w_rg_x, b_rg_x, rg_lambda,
           w_branch_a, w_branch_b, w_out, ln1_g, ln1_b):
    o_c = Q_DIM
    o_qi = o_c + KV_LATENT
    o_ki = o_qi + IDX_Q_DIM
    o_xr = o_ki + IDX_DIM + IDX_HEADS
    wb = w_in.astype(BF16)
    w_qq = jnp.concatenate([wb[:, :o_c], wb[:, o_qi:o_ki]], axis=1)
    w_kw = jnp.pad(wb[:, o_ki:o_xr], ((0, 0), (0, LANES - IDX_DIM - IDX_HEADS)))
    qq = matmul(h0b, w_qq, BF16, 1024, 512, "proj_q")
    c = matmul_rms(h0b, wb[:, o_c:o_qi], kv_norm_g)
    kw = matmul(h0b, w_kw, F32, 1024, LANES, "proj_idx_k")
    zz = matmul(h0b, wb[:, o_xr:], F32, 1024, 512, "proj_lru_gates")
    attn = dsa_attention(qq, kw, c, w_uk, w_uv)
    lru = rg_lru(zz, conv_w, conv_b, w_rg_a, b_rg_a, w_rg_x, b_rg_x, rg_lambda)
    merged = merge_branches(attn, lru, w_branch_a.astype(BF16), w_branch_b.astype(BF16), zz)
    return out_proj_ln(merged, w_out.astype(BF16), h0f, ln1_g, ln1_b)


def _moe(h1, w_router, router_bias, w_gate_e, w_up_e, w_down_e, w_gate_s, w_up_s, w_down_s, ln2_g, ln2_b):
    T = h1.shape[0]
    eidx, gate, pos, cnt = router(h1, w_router, router_bias)
    counts = cnt[0].astype(I32)
    starts = jnp.cumsum(counts) - counts
    dest = (starts[eidx] + pos).reshape(T * TOP_K)
    xs = dispatch(dest, h1)
    ys = expert_ffn(visit_tables(counts, T * TOP_K), xs, w_gate_e, w_up_e, w_down_e)
    return combine(dest, gate, h1, w_gate_s, w_up_s, w_down_s, ln2_g, ln2_b, ys)


def kernel(x, ln_in_g, ln_in_b, w_in, kv_norm_g, w_uk, w_uv, conv_w, conv_b, w_rg_a, b_rg_a, w_rg_x, b_rg_x,
           rg_lambda, w_branch_a, w_branch_b, w_out, ln1_g, ln1_b, w_router, router_bias, w_gate_e, w_up_e,
           w_down_e, w_gate_s, w_up_s, w_down_s, ln2_g, ln2_b):
    B, T, D = x.shape
    assert B == 1 and D == D_MODEL and w_in.shape[0] == DEPTH
    hf, hb = ln_in(x.reshape(T, D), ln_in_g, ln_in_b)
    for l in range(DEPTH):
        h1 = _mixer(hf, hb, w_in[l], kv_norm_g[l], w_uk[l], w_uv[l], conv_w[l], conv_b[l], w_rg_a[l], b_rg_a[l],
                    w_rg_x[l], b_rg_x[l], rg_lambda[l], w_branch_a[l], w_branch_b[l], w_out[l], ln1_g[l], ln1_b[l])
        hf = _moe(h1, w_router[l], router_bias[l], w_gate_e[l], w_up_e[l], w_down_e[l], w_gate_s[l], w_up_s[l],
                  w_down_s[l], ln2_g[l], ln2_b[l])
        hb = hf.astype(BF16)
    return hf.reshape(B, T, D)
```

```python
import functools

import jax
import jax.numpy as jnp
from jax import lax
from jax.experimental import pallas as pl
from jax.experimental.pallas import tpu as pltpu

F32 = jnp.float32
BF16 = jnp.bfloat16
I32 = jnp.int32

D_MODEL = 2048
CHUNK = 64
N_HEADS = 16
HEAD_DIM = 128
KV_LATENT = 256
IDX_HEADS = 16
IDX_DIM = 64
TOPK_KEYS_MAX = 256
ATTN_SCALE = HEAD_DIM ** -0.5
LRU_WIDTH = 2048
LRU_BLOCKS = 16
LRU_BLOCK_DIM = LRU_WIDTH // LRU_BLOCKS
CONV_WIDTH = 4
RG_C = 8.0
N_EXPERTS = 64
TOP_K = 8
N_GROUPS = 8
TOPK_GROUPS = 4
EXPERT_FF = 512
SHARED_FF = 512
ROUTE_SCALE = 2.5
LN_EPS = 1e-5
RMS_EPS = 1e-6
Q_DIM = N_HEADS * HEAD_DIM
IDX_Q_DIM = IDX_HEADS * IDX_DIM
DEPTH = 1
ALPHA = (2.0 * DEPTH) ** 0.25

VMEM_LIMIT_V7X = 56 * 1024 * 1024
LANES = 128

ATTN_QB = 128
ATTN_TK = 512
EXPERT_TILE = 128
ROW_TILE = 128

NT_DIMS = (((1,), (1,)), ((), ()))


def _cparams(n_axes=1, vmem=VMEM_LIMIT_V7X):
    return pltpu.CompilerParams(dimension_semantics=("arbitrary",) * n_axes, vmem_limit_bytes=vmem)


def _layer_norm_rows(x, g, b):
    mu = jnp.mean(x, axis=-1, keepdims=True)
    xc = x - mu
    var = jnp.mean(xc * xc, axis=-1, keepdims=True)
    return xc * lax.rsqrt(var + LN_EPS) * g + b


def _sigmoid(x):
    return 1.0 / (1.0 + jnp.exp(-x))


def _ln_in_kernel(x_ref, g_ref, b_ref, hf_ref, hb_ref):
    y = _layer_norm_rows(x_ref[...], g_ref[...], b_ref[...])
    hf_ref[...] = y
    hb_ref[...] = y.astype(BF16)


def ln_in(x, g, b, tm=512):
    T, D = x.shape
    row = pl.BlockSpec((tm, D), lambda i: (i, 0))
    vec = pl.BlockSpec((1, D), lambda i: (0, 0))
    return pl.pallas_call(
        _ln_in_kernel, grid=(T // tm,), in_specs=[row, vec, vec], out_specs=[row, row],
        out_shape=[jax.ShapeDtypeStruct((T, D), F32), jax.ShapeDtypeStruct((T, D), BF16)],
        compiler_params=_cparams(), name="ln_in")(x, g.reshape(1, D), b.reshape(1, D))


def _mm_kernel(x_ref, w_ref, o_ref):
    o_ref[...] = jnp.dot(x_ref[...], w_ref[...], preferred_element_type=F32).astype(o_ref.dtype)


def matmul(x, w, out_dtype, tm, tn, name):
    M, K = x.shape
    N = w.shape[1]
    return pl.pallas_call(
        _mm_kernel, grid=(M // tm, N // tn),
        in_specs=[pl.BlockSpec((tm, K), lambda i, j: (i, 0)), pl.BlockSpec((K, tn), lambda i, j: (0, j))],
        out_specs=pl.BlockSpec((tm, tn), lambda i, j: (i, j)),
        out_shape=jax.ShapeDtypeStruct((M, N), out_dtype),
        compiler_params=_cparams(2), name=name)(x, w)


def _mm_rms_kernel(x_ref, w_ref, g_ref, o_ref):
    c = jnp.dot(x_ref[...], w_ref[...], preferred_element_type=F32)
    y = c * lax.rsqrt(jnp.mean(c * c, axis=-1, keepdims=True) + RMS_EPS) * g_ref[...]
    o_ref[...] = y.astype(o_ref.dtype)


def matmul_rms(x, w, g, tm=1024):
    M, K = x.shape
    N = w.shape[1]
    return pl.pallas_call(
        _mm_rms_kernel, grid=(M // tm,),
        in_specs=[pl.BlockSpec((tm, K), lambda i: (i, 0)), pl.BlockSpec((K, N), lambda i: (0, 0)),
                  pl.BlockSpec((1, N), lambda i: (0, 0))],
        out_specs=pl.BlockSpec((tm, N), lambda i: (i, 0)),
        out_shape=jax.ShapeDtypeStruct((M, N), BF16),
        compiler_params=_cparams(), name="proj_kv_rms")(x, w, g.reshape(1, N))


def _attn_kernel(q_ref, qi_ref, wt_ref, kidx_ref, c_ref, ct_ref, wukt_ref, wuv_ref, o_ref,
                 sc_ref, big_ref, bias_ref, acc_ref, m_ref, l_ref, qlt_ref, qis_ref, thr_ref, j_ref,
                 *, QB, TK, NSEL):
    H = N_HEADS
    NP = H // 2
    b = pl.program_id(0)
    lmax = (b + 1) * QB
    nk = lax.div(lmax + (TK - 1), TK)
    ksel = float(NSEL)
    neg_inf = float("-inf")
    fmax = float(jnp.finfo(jnp.float32).max)

    tpos = b * QB + lax.broadcasted_iota(I32, (1, QB), 1)
    limit = (lax.shift_right_logical(tpos, 6) + 1) * CHUNK
    limit_f = limit.astype(F32)

    for h in range(H):
        qis_ref[h * QB:(h + 1) * QB, :] = qi_ref[:, h * IDX_DIM:(h + 1) * IDX_DIM]
        qlat_t = lax.dot_general(wukt_ref[h], q_ref[:, h * HEAD_DIM:(h + 1) * HEAD_DIM], NT_DIMS,
                                 preferred_element_type=F32)
        qlt_ref[h // 2, :, (h % 2) * QB:(h % 2 + 1) * QB] = (qlat_t * ATTN_SCALE).astype(BF16)
    wscale = (IDX_HEADS ** -0.5) * (IDX_DIM ** -0.5)
    wrows = [wt_ref[h:h + 1, :] * wscale for h in range(H)]

    RC = 128

    def p1(kt, carry):
        k0 = pl.multiple_of(kt * TK, TK)
        big_ref[...] = lax.dot_general(kidx_ref[pl.ds(k0, TK), :], qis_ref[...], NT_DIMS,
                                       preferred_element_type=F32)
        for r in range(TK // RC):
            rows = slice(r * RC, (r + 1) * RC)
            acc = jnp.zeros((RC, QB), F32)
            for h in range(H):
                acc = acc + wrows[h] * jnp.maximum(big_ref[rows, h * QB:(h + 1) * QB], 0.0)
            key = k0 + r * RC + lax.broadcasted_iota(I32, (RC, QB), 0)
            sc_ref[kt, rows, :] = jnp.where(key < limit, acc, neg_inf)
        return carry

    lax.fori_loop(0, nk, p1, 0)

    CR = 32

    def count_ge(t):
        tb = jnp.broadcast_to(t, (CR, QB))

        def body(kt, cnt):
            hit = jnp.where(sc_ref[kt].reshape(TK // CR, CR, QB) >= tb, 1.0, 0.0)
            return cnt + jnp.sum(hit, axis=0)

        cnt = lax.fori_loop(0, nk, body, jnp.zeros((CR, QB), F32))
        return jnp.sum(cnt, axis=0, keepdims=True)

    def minmax_body(kt, carry):
        mn, mx = carry
        x = sc_ref[kt]
        mx = jnp.maximum(mx, jnp.max(x, axis=0, keepdims=True))
        mn = jnp.minimum(mn, jnp.min(jnp.where(x == neg_inf, fmax, x), axis=0, keepdims=True))
        return mn, mx

    mn, mx = lax.fori_loop(0, nk, minmax_body,
                           (jnp.full((1, QB), fmax, F32), jnp.full((1, QB), neg_inf, F32)))

    c_top = count_ge(mx)
    one = jnp.ones((1, QB), F32)
    zero = jnp.zeros((1, QB), F32)
    all_sel = jnp.where(limit_f <= ksel, one, zero)
    top_tie = (1.0 - all_sel) * jnp.where(c_top >= ksel, one, zero)
    lo0 = jnp.where(top_tie > 0, mx, mn)
    c_hi0 = jnp.where(top_tie > 0, zero, c_top)
    thr0 = jnp.where(all_sel > 0, -fmax, mx)
    tie0 = top_tie * jnp.where(c_top > ksel, one, zero)
    done0 = jnp.maximum(all_sel, top_tie)

    def bis_cond(st):
        return jnp.logical_and(st[0] > 0.5, st[1] < 400)

    def bis_body(st):
        _, it, lo, hi, c_hi, thr, tie, done = st
        act = done < 0.5
        mid = 0.5 * lo + 0.5 * hi
        adjacent = jnp.logical_or(mid <= lo, mid >= hi)
        c = count_ge(mid)
        ge = c >= ksel
        exact = c == ksel
        stop = jnp.logical_or(exact, adjacent)
        fin_tie = jnp.logical_and(act, jnp.logical_and(adjacent, jnp.logical_not(exact)))
        thr = jnp.where(jnp.logical_and(act, exact), mid, jnp.where(fin_tie, lo, thr))
        tie = jnp.where(fin_tie, one, tie)
        upd = jnp.logical_and(act, jnp.logical_not(stop))
        up_lo = jnp.logical_and(upd, ge)
        up_hi = jnp.logical_and(upd, jnp.logical_not(ge))
        lo = jnp.where(up_lo, mid, lo)
        hi = jnp.where(up_hi, mid, hi)
        c_hi = jnp.where(up_hi, c, c_hi)
        done = jnp.where(jnp.logical_and(act, stop), one, done)
        return jnp.max(1.0 - done), it + 1, lo, hi, c_hi, thr, tie, done

    st = lax.while_loop(bis_cond, bis_body,
                        (jnp.max(1.0 - done0), jnp.int32(0), lo0, mx, c_hi0, thr0, tie0, done0))
    _, _, _, _, c_hi, thr, tie, _ = st
    thr_ref[...] = thr
    big_j = float(2 ** 30)
    j_ref[...] = jnp.full((1, QB), big_j, F32)

    @pl.when(jnp.max(tie) > 0.5)
    def _():
        need = ksel - c_hi

        def count_tie_below(jb):
            def body(kt, cnt):
                keyf = (kt * TK + lax.broadcasted_iota(I32, (TK, QB), 0)).astype(F32)
                hit = jnp.logical_and(sc_ref[kt] == thr, keyf < jb)
                return cnt + jnp.sum(jnp.where(hit, 1.0, 0.0), axis=0, keepdims=True)

            return lax.fori_loop(0, nk, body, jnp.zeros((1, QB), F32))

        def jbody(_, carry):
            jlo, jhi = carry
            jm = jnp.floor((jlo + jhi) * 0.5)
            ok = count_tie_below(jm) >= need
            return jnp.where(ok, jlo, jm), jnp.where(ok, jm, jhi)

        n_steps = max(1, int(sc_ref.shape[0] * TK).bit_length())
        _, jhi = lax.fori_loop(0, n_steps, jbody, (zero, zero + (nk * TK).astype(F32)))
        j_ref[...] = jnp.where(tie > 0.5, jhi, big_j)

    m_ref[...] = jnp.full(m_ref.shape, -1e30, F32)
    l_ref[...] = jnp.zeros(l_ref.shape, F32)
    acc_ref[...] = jnp.zeros(acc_ref.shape, F32)

    def p3(kt, carry):
        k0 = pl.multiple_of(kt * TK, TK)
        ck = c_ref[pl.ds(k0, TK), :]
        ckt = ct_ref[kt]
        x = sc_ref[kt]
        keyf = (k0 + lax.broadcasted_iota(I32, (TK, QB), 0)).astype(F32)
        t = thr_ref[...]
        sel = jnp.logical_or(x > t, jnp.logical_and(x == t, keyf < j_ref[...]))
        bias_ref[...] = jnp.where(sel, 0.0, neg_inf)
        for pr in range(NP):
            s = jnp.dot(ck, qlt_ref[pr], preferred_element_type=F32)
            bias = bias_ref[...]
            s = jnp.concatenate([s[:, :QB] + bias, s[:, QB:] + bias], axis=1)
            m_old = m_ref[pr]
            m_new = jnp.maximum(m_old, jnp.max(s, axis=0, keepdims=True))
            p = jnp.exp(s - m_new)
            alpha = jnp.exp(m_old - m_new)
            l_ref[pr] = alpha * l_ref[pr] + jnp.sum(p, axis=0, keepdims=True)
            m_ref[pr] = m_new
            pv = jnp.dot(ckt, p.astype(BF16), preferred_element_type=F32)
            acc_ref[pr] = alpha * acc_ref[pr] + pv
        return carry

    lax.fori_loop(0, nk, p3, 0)

    for h in range(H):
        cols = slice((h % 2) * QB, (h % 2 + 1) * QB)
        o_lat_t = acc_ref[h // 2, :, cols] / l_ref[h // 2, :, cols]
        o_ref[:, h * HEAD_DIM:(h + 1) * HEAD_DIM] = jnp.dot(
            o_lat_t.T.astype(BF16), wuv_ref[h], preferred_element_type=F32).astype(o_ref.dtype)


def dsa_attention(qq, kw, c, w_uk, w_uv, qb=ATTN_QB, tk=ATTN_TK):
    T = c.shape[0]
    tk = min(tk, T)
    nsel = min(TOPK_KEYS_MAX, T // 4)
    assert qb % CHUNK == 0 and T % tk == 0 and T % qb == 0 and tk % qb == 0
    H = N_HEADS
    kidx = kw[:, :IDX_DIM].astype(BF16)
    w_t = kw[:, IDX_DIM:IDX_DIM + IDX_HEADS].T
    c_t = c.reshape(T // tk, tk, KV_LATENT).transpose(0, 2, 1)
    kernel = functools.partial(_attn_kernel, QB=qb, TK=tk, NSEL=nsel)
    full = lambda shape: pl.BlockSpec(shape, lambda i: (0,) * len(shape))
    return pl.pallas_call(
        kernel, grid=(T // qb,),
        in_specs=[pl.BlockSpec((qb, Q_DIM), lambda i: (i, 0)),
                  pl.BlockSpec((qb, IDX_Q_DIM), lambda i: (i, Q_DIM // IDX_Q_DIM)),
                  pl.BlockSpec((H, qb), lambda i: (0, i)),
                  full((T, IDX_DIM)), full((T, KV_LATENT)), full((T // tk, KV_LATENT, tk)),
                  full((H, KV_LATENT, HEAD_DIM)), full((H, KV_LATENT, HEAD_DIM))],
        out_specs=pl.BlockSpec((qb, Q_DIM), lambda i: (i, 0)),
        out_shape=jax.ShapeDtypeStruct((T, Q_DIM), BF16),
        scratch_shapes=[pltpu.VMEM((T // tk, tk, qb), F32),
                        pltpu.VMEM((tk, H * qb), F32),
                        pltpu.VMEM((tk, qb), F32),
                        pltpu.VMEM((H // 2, KV_LATENT, 2 * qb), F32),
                        pltpu.VMEM((H // 2, 1, 2 * qb), F32), pltpu.VMEM((H // 2, 1, 2 * qb), F32),
                        pltpu.VMEM((H // 2, KV_LATENT, 2 * qb), BF16),
                        pltpu.VMEM((H * qb, IDX_DIM), BF16),
                        pltpu.VMEM((1, qb), F32), pltpu.VMEM((1, qb), F32)],
        compiler_params=_cparams(), name="dsa_attention")(
            qq, qq, w_t, kidx, c, c_t, w_uk.transpose(0, 2, 1).astype(BF16), w_uv.astype(BF16))


def _gelu_tanh(x):
    return 0.5 * x * (1.0 + jnp.tanh(0.7978845608028654 * (x + 0.044715 * (x * x * x))))


def _lru_kernel(xr_ref, yg_ref, cw_ref, cb_ref, wa_ref, ba_ref, wx_ref, bx_ref, lam_ref, o_ref,
                ext_ref, a_ref, u_ref, hs_ref, h_ref, *, TT):
    i = pl.program_id(0)
    C = LRU_WIDTH

    @pl.when(i == 0)
    def _():
        ext_ref[0:8, :] = jnp.zeros((8, C), F32)
        h_ref[...] = jnp.zeros((1, C), F32)

    ext_ref[8:8 + TT, :] = xr_ref[...]
    xc = jnp.zeros((TT, C), F32) + cb_ref[...]
    for j in range(CONV_WIDTH):
        off = 8 - (CONV_WIDTH - 1) + j
        xc = xc + cw_ref[j:j + 1, :] * ext_ref[off:off + TT, :]
    ext_ref[0:8, :] = ext_ref[TT:TT + 8, :]

    lam = lam_ref[...]
    softplus_neg = jnp.maximum(-lam, 0.0) + jnp.log1p(jnp.exp(-jnp.abs(lam)))
    xcb = xc.astype(BF16)
    for n in range(LRU_BLOCKS):
        cols = slice(n * LRU_BLOCK_DIM, (n + 1) * LRU_BLOCK_DIM)
        xb = xcb[:, cols]
        r = _sigmoid(jnp.dot(xb, wa_ref[n], preferred_element_type=F32) + ba_ref[:, cols])
        g = _sigmoid(jnp.dot(xb, wx_ref[n], preferred_element_type=F32) + bx_ref[:, cols])
        log_a = (-RG_C) * r * softplus_neg[:, cols]
        a_ref[:, cols] = jnp.exp(log_a)
        th = jnp.tanh(log_a)
        u_ref[:, cols] = jnp.sqrt((-2.0 * th) / (1.0 - th)) * (g * xc[:, cols])

    def step(t, h):
        h = a_ref[pl.ds(t, 1), :] * h + u_ref[pl.ds(t, 1), :]
        hs_ref[pl.ds(t, 1), :] = h
        return h

    h_ref[...] = lax.fori_loop(0, TT, step, h_ref[...], unroll=8)
    o_ref[...] = (hs_ref[...] * _gelu_tanh(yg_ref[...])).astype(o_ref.dtype)


def rg_lru(zz, conv_w, conv_b, w_a, b_a, w_x, b_x, lam, tt=256):
    T = zz.shape[0]
    tt = min(tt, T)
    C = LRU_WIDTH
    vec = pl.BlockSpec((1, C), lambda i: (0, 0))
    blk = pl.BlockSpec((LRU_BLOCKS, LRU_BLOCK_DIM, LRU_BLOCK_DIM), lambda i: (0, 0, 0))
    return pl.pallas_call(
        functools.partial(_lru_kernel, TT=tt), grid=(T // tt,),
        in_specs=[pl.BlockSpec((tt, C), lambda i: (i, 0)), pl.BlockSpec((tt, C), lambda i: (i, 1)),
                  pl.BlockSpec((CONV_WIDTH, C), lambda i: (0, 0)), vec, blk, vec, blk, vec, vec],
        out_specs=pl.BlockSpec((tt, C), lambda i: (i, 0)),
        out_shape=jax.ShapeDtypeStruct((T, C), BF16),
        scratch_shapes=[pltpu.VMEM((tt + 8, C), F32), pltpu.VMEM((tt, C), F32), pltpu.VMEM((tt, C), F32),
                        pltpu.VMEM((tt, C), F32), pltpu.VMEM((1, C), F32)],
        compiler_params=_cparams(), name="rg_lru")(
            zz, zz, conv_w, conv_b.reshape(1, C), w_a.astype(BF16), b_a.reshape(1, C),
            w_x.astype(BF16), b_x.reshape(1, C), lam.reshape(1, C))


def _merge_kernel(a_ref, l_ref, wa_ref, wb_ref, ga_ref, gb_ref, o_ref):
    ya = jnp.dot(a_ref[...], wa_ref[...], preferred_element_type=F32)
    yb = jnp.dot(l_ref[...], wb_ref[...], preferred_element_type=F32)
    o_ref[...] = (_sigmoid(ga_ref[...]) * ya + _sigmoid(gb_ref[...]) * yb).astype(o_ref.dtype)


def merge_branches(attn, lru, w_a, w_b, zz, tm=512, tn=512):
    T, D = attn.shape
    nb = D // tn
    return pl.pallas_call(
        _merge_kernel, grid=(T // tm, nb),
        in_specs=[pl.BlockSpec((tm, D), lambda i, j: (i, 0)), pl.BlockSpec((tm, D), lambda i, j: (i, 0)),
                  pl.BlockSpec((D, tn), lambda i, j: (0, j)), pl.BlockSpec((D, tn), lambda i, j: (0, j)),
                  pl.BlockSpec((tm, tn), lambda i, j: (i, 2 * nb + j)),
                  pl.BlockSpec((tm, tn), lambda i, j: (i, 3 * nb + j))],
        out_specs=pl.BlockSpec((tm, tn), lambda i, j: (i, j)),
        out_shape=jax.ShapeDtypeStruct((T, D), BF16),
        compiler_params=_cparams(2), name="merge_branches")(attn, lru, w_a, w_b, zz, zz)


def _post1_kernel(m_ref, w_ref, h_ref, g_ref, b_ref, o_ref):
    y = ALPHA * h_ref[...] + jnp.dot(m_ref[...], w_ref[...], preferred_element_type=F32)
    o_ref[...] = _layer_norm_rows(y, g_ref[...], b_ref[...])


def out_proj_ln(merged, w_out, h0, g, b, tm=512):
    T, D = merged.shape
    row = lambda dt: pl.BlockSpec((tm, D), lambda i: (i, 0))
    vec = pl.BlockSpec((1, D), lambda i: (0, 0))
    return pl.pallas_call(
        _post1_kernel, grid=(T // tm,),
        in_specs=[row(BF16), pl.BlockSpec((D, D), lambda i: (0, 0)), row(F32), vec, vec],
        out_specs=row(F32), out_shape=jax.ShapeDtypeStruct((T, D), F32),
        compiler_params=_cparams(), name="out_proj_ln1")(merged, w_out, h0, g.reshape(1, D), b.reshape(1, D))


def _router_kernel(h_ref, wr_ref, rb_ref, eidx_ref, gate_ref, pos_ref, cnt_ref, carry_ref, *, TM):
    i = pl.program_id(0)
    E = N_EXPERTS
    per_group = E // N_GROUPS
    neg_inf = float("-inf")

    @pl.when(i == 0)
    def _():
        carry_ref[...] = jnp.zeros((1, E), F32)

    h = h_ref[...]
    hh = h.astype(BF16)
    hl = (h - hh.astype(F32)).astype(BF16)
    w = wr_ref[...]
    wh = w.astype(BF16)
    wl = (w - wh.astype(F32)).astype(BF16)
    logits = (jnp.dot(hh, wh, preferred_element_type=F32)
              + (jnp.dot(hh, wl, preferred_element_type=F32) + jnp.dot(hl, wh, preferred_element_type=F32)))
    scores = _sigmoid(logits)
    biased = scores + rb_ref[...]
    lane = lax.broadcasted_iota(I32, (TM, E), 1)

    def first_argmax(v):
        m = jnp.max(v, axis=1, keepdims=True)
        idx = jnp.min(jnp.where(v == m, lane, E), axis=1, keepdims=True)
        return m, idx

    gscore = []
    for g in range(N_GROUPS):
        in_g = jnp.logical_and(lane >= g * per_group, lane < (g + 1) * per_group)
        vg = jnp.where(in_g, biased, neg_inf)
        m1, i1 = first_argmax(vg)
        m2 = jnp.max(jnp.where(lane == i1, neg_inf, vg), axis=1, keepdims=True)
        gscore.append(m1 + m2)
    cand = jnp.full((TM, E), neg_inf, F32)
    for g in range(N_GROUPS):
        rank = jnp.zeros((TM, 1), F32)
        for g2 in range(N_GROUPS):
            if g2 == g:
                continue
            ahead = (gscore[g2] >= gscore[g]) if g2 < g else (gscore[g2] > gscore[g])
            rank = rank + jnp.where(ahead, 1.0, 0.0)
        in_g = jnp.logical_and(lane >= g * per_group, lane < (g + 1) * per_group)
        keep = jnp.logical_and(in_g, rank < float(TOPK_GROUPS))
        cand = jnp.where(keep, biased, cand)

    lane_k = lax.broadcasted_iota(I32, (TM, TOP_K), 1)
    sel = jnp.zeros((TM, E), F32)
    onehots, gates = [], []
    eidx = jnp.zeros((TM, TOP_K), I32)
    for k in range(TOP_K):
        _, ik = first_argmax(cand)
        oh = lane == ik
        onehots.append(oh)
        gates.append(jnp.sum(jnp.where(oh, scores, 0.0), axis=1, keepdims=True))
        cand = jnp.where(oh, neg_inf, cand)
        sel = sel + jnp.where(oh, 1.0, 0.0)
        eidx = jnp.where(lane_k == k, ik, eidx)
    gsum = gates[0]
    for k in range(1, TOP_K):
        gsum = gsum + gates[k]

    r_i = lax.broadcasted_iota(I32, (TM, TM), 0)
    c_i = lax.broadcasted_iota(I32, (TM, TM), 1)
    tri = jnp.where(c_i < r_i, 1.0, 0.0).astype(BF16)
    prefix = jnp.dot(tri, sel.astype(BF16), preferred_element_type=F32) + carry_ref[...]
    gate = jnp.zeros((TM, TOP_K), F32)
    pos = jnp.zeros((TM, TOP_K), F32)
    for k in range(TOP_K):
        gk = gates[k] / gsum * ROUTE_SCALE
        pk = jnp.sum(jnp.where(onehots[k], prefix, 0.0), axis=1, keepdims=True)
        gate = jnp.where(lane_k == k, gk, gate)
        pos = jnp.where(lane_k == k, pk, pos)
    eidx_ref[...] = eidx
    gate_ref[...] = gate
    pos_ref[...] = pos.astype(I32)
    carry_ref[...] = carry_ref[...] + jnp.sum(sel, axis=0, keepdims=True)
    cnt_ref[...] = carry_ref[...]


def router(h1, w_router, router_bias, tm=512):
    T, D = h1.shape
    tm = min(tm, T)
    E = N_EXPERTS
    out8 = pl.BlockSpec((tm, TOP_K), lambda i: (i, 0))
    return pl.pallas_call(
        functools.partial(_router_kernel, TM=tm), grid=(T // tm,),
        in_specs=[pl.BlockSpec((tm, D), lambda i: (i, 0)), pl.BlockSpec((D, E), lambda i: (0, 0)),
                  pl.BlockSpec((1, E), lambda i: (0, 0))],
        out_specs=[out8, out8, out8, pl.BlockSpec((1, E), lambda i: (0, 0))],
        out_shape=[jax.ShapeDtypeStruct((T, TOP_K), I32), jax.ShapeDtypeStruct((T, TOP_K), F32),
                   jax.ShapeDtypeStruct((T, TOP_K), I32), jax.ShapeDtypeStruct((1, E), F32)],
        scratch_shapes=[pltpu.VMEM((1, E), F32)],
        compiler_params=_cparams(), name="router")(h1, w_router, router_bias.reshape(1, E))


def _row_copy(src, src_row, dst, dst_row, sem):
    return pltpu.make_async_copy(src.at[pl.ds(src_row, 1)], dst.at[pl.ds(dst_row, 1)], sem)


def _dest_row(starts_ref, eidx_ref, pos_ref, i):
    return starts_ref[eidx_ref[i]] + pos_ref[i]


def _dispatch_kernel(starts_ref, eidx_ref, pos_ref, h_ref, xs_hbm, sem, *, TM):
    def start(n, carry):
        for k in range(TOP_K):
            d = _dest_row(starts_ref, eidx_ref, pos_ref, n * TOP_K + k)
            _row_copy(h_ref, n, xs_hbm, d, sem).start()
        return carry

    def wait(n, carry):
        for k in range(TOP_K):
            _row_copy(h_ref, 0, xs_hbm, 0, sem).wait()
        return carry

    lax.fori_loop(0, TM, start, 0)
    lax.fori_loop(0, TM, wait, 0)


def _route_specs(tm):
    idx = pl.BlockSpec((tm * TOP_K,), lambda i: (i,), memory_space=pltpu.SMEM)
    return [pl.BlockSpec((N_EXPERTS,), lambda i: (0,), memory_space=pltpu.SMEM), idx, idx]


def dispatch(starts, eidx, pos, h1, tm=ROW_TILE):
    T, D = h1.shape
    return pl.pallas_call(
        functools.partial(_dispatch_kernel, TM=tm), grid=(T // tm,),
        in_specs=_route_specs(tm) + [pl.BlockSpec((tm, D), lambda i: (i, 0))],
        out_specs=pl.BlockSpec(memory_space=pl.ANY),
        out_shape=jax.ShapeDtypeStruct((T * TOP_K, D), h1.dtype),
        scratch_shapes=[pltpu.SemaphoreType.DMA(())],
        compiler_params=_cparams(), name="moe_dispatch")(starts, eidx, pos, h1)


def _expert_kernel(vt_ref, ve_ref, vlo_ref, vhi_ref, vfl_ref, x_ref, wg_ref, wu_ref, wd_ref, o_ref,
                   wgb_ref, wub_ref, wdb_ref, *, TE):
    v = pl.program_id(0)
    flags = vfl_ref[v]

    @pl.when((flags & 2) != 0)
    def _():
        wgb_ref[...] = wg_ref[0].astype(BF16)
        wub_ref[...] = wu_ref[0].astype(BF16)
        wdb_ref[...] = wd_ref[0].astype(BF16)

    @pl.when((flags & 4) != 0)
    def _():
        x = x_ref[...].astype(BF16)
        g = jnp.dot(x, wgb_ref[...], preferred_element_type=F32)
        u = jnp.dot(x, wub_ref[...], preferred_element_type=F32)
        rows = vt_ref[v] * TE + lax.broadcasted_iota(I32, (TE, 1), 0)
        mine = jnp.logical_and(rows >= vlo_ref[v], rows < vhi_ref[v])
        mid = jnp.where(mine, (g * _sigmoid(g)) * u, 0.0).astype(BF16)
        y = jnp.dot(mid, wdb_ref[...], preferred_element_type=F32)

        @pl.when((flags & 1) != 0)
        def _():
            o_ref[...] = y

        @pl.when((flags & 1) == 0)
        def _():
            o_ref[...] = o_ref[...] + y


def expert_ffn(tables, xs, w_gate_e, w_up_e, w_down_e, te=EXPERT_TILE):
    R, D = xs.shape
    F = EXPERT_FF
    nv = tables[0].shape[0]
    grid_spec = pltpu.PrefetchScalarGridSpec(
        num_scalar_prefetch=5, grid=(nv,),
        in_specs=[pl.BlockSpec((te, D), lambda v, vt, ve, lo, hi, fl: (vt[v], 0)),
                  pl.BlockSpec((1, D, F), lambda v, vt, ve, lo, hi, fl: (ve[v], 0, 0)),
                  pl.BlockSpec((1, D, F), lambda v, vt, ve, lo, hi, fl: (ve[v], 0, 0)),
                  pl.BlockSpec((1, F, D), lambda v, vt, ve, lo, hi, fl: (ve[v], 0, 0))],
        out_specs=pl.BlockSpec((te, D), lambda v, vt, ve, lo, hi, fl: (vt[v], 0)),
        scratch_shapes=[pltpu.VMEM((D, F), BF16), pltpu.VMEM((D, F), BF16), pltpu.VMEM((F, D), BF16)])
    return pl.pallas_call(
        functools.partial(_expert_kernel, TE=te), grid_spec=grid_spec,
        out_shape=jax.ShapeDtypeStruct((R, D), F32),
        compiler_params=_cparams(), name="expert_ffn")(*tables, xs, w_gate_e, w_up_e, w_down_e)


def visit_tables(counts, n_rows, te=EXPERT_TILE):
    E = counts.shape[0]
    n_tiles = n_rows // te
    nv = n_tiles + E
    ends = jnp.cumsum(counts)
    starts = ends - counts
    first_tile = starts // te
    last_tile = jnp.maximum(ends - 1, 0) // te
    nvis = jnp.where(counts > 0, last_tile - first_tile + 1, 0)
    vis_end = jnp.cumsum(nvis)
    vis_start = vis_end - nvis
    total = vis_end[-1]
    v = jnp.arange(nv, dtype=I32)
    vc = jnp.minimum(v, total - 1)
    e = jnp.sum((vis_end[None, :] <= vc[:, None]).astype(I32), axis=1)
    tile = (first_tile[e] + vc - vis_start[e]).astype(I32)
    lo = jnp.maximum(starts[e], tile * te).astype(I32)
    hi = jnp.minimum(ends[e], (tile + 1) * te).astype(I32)
    active = v < total
    prev_tile = jnp.concatenate([jnp.full((1,), -1, I32), tile[:-1]])
    prev_e = jnp.concatenate([jnp.full((1,), -1, I32), e[:-1]])
    flags = jnp.where(active, 4 + jnp.where(tile != prev_tile, 1, 0) + jnp.where(e != prev_e, 2, 0), 0).astype(I32)
    return tile, e, lo, hi, flags


def _combine_kernel(starts_ref, eidx_ref, pos_ref, gate_ref, h_ref, wgs_ref, wus_ref, wds_ref, g_ref, b_ref,
                    ys_hbm, o_ref, buf_ref, sem, *, TM):
    def start(n, carry):
        for k in range(TOP_K):
            d = _dest_row(starts_ref, eidx_ref, pos_ref, n * TOP_K + k)
            _row_copy(ys_hbm, d, buf_ref.at[k], n, sem).start()
        return carry

    def wait(n, carry):
        for k in range(TOP_K):
            _row_copy(ys_hbm, 0, buf_ref.at[k], 0, sem).wait()
        return carry

    lax.fori_loop(0, TM, start, 0)
    h = h_ref[...]
    hb = h.astype(BF16)
    g = jnp.dot(hb, wgs_ref[...], preferred_element_type=F32)
    u = jnp.dot(hb, wus_ref[...], preferred_element_type=F32)
    shared = jnp.dot(((g * _sigmoid(g)) * u).astype(BF16), wds_ref[...], preferred_element_type=F32)
    lax.fori_loop(0, TM, wait, 0)
    routed = jnp.zeros(h.shape, F32)
    for k in range(TOP_K):
        routed = routed + buf_ref[k] * gate_ref[:, k:k + 1]
    y = ALPHA * h + (routed + shared)
    o_ref[...] = _layer_norm_rows(y, g_ref[...], b_ref[...])


def combine(starts, eidx, pos, gate, h1, w_gate_s, w_up_s, w_down_s, g, b, ys, tm=ROW_TILE):
    T, D = h1.shape
    F = SHARED_FF
    row = pl.BlockSpec((tm, D), lambda i: (i, 0))
    vec = pl.BlockSpec((1, D), lambda i: (0, 0))
    return pl.pallas_call(
        functools.partial(_combine_kernel, TM=tm), grid=(T // tm,),
        in_specs=_route_specs(tm) + [
            pl.BlockSpec((tm, TOP_K), lambda i: (i, 0)), row,
            pl.BlockSpec((D, F), lambda i: (0, 0)), pl.BlockSpec((D, F), lambda i: (0, 0)),
            pl.BlockSpec((F, D), lambda i: (0, 0)), vec, vec,
            pl.BlockSpec(memory_space=pl.ANY)],
        out_specs=row, out_shape=jax.ShapeDtypeStruct((T, D), F32),
        scratch_shapes=[pltpu.VMEM((TOP_K, tm, D), F32), pltpu.SemaphoreType.DMA(())],
        compiler_params=_cparams(), name="moe_combine_ln2")(
            starts, eidx, pos, gate, h1, w_gate_s.astype(BF16), w_up_s.astype(BF16), w_down_s.astype(BF16),
            g.reshape(1, D), b.reshape(1, D), ys)


def _mixer(h0f, h0b, w_in, kv_norm_g, w_uk, w_uv, conv_w, conv_b, w_rg_a, b_rg_a, w_rg_x, b_rg_x, rg_lambda,
           w_branch_a, w_branch_b, w_out, ln1_g, ln1_b):
    o_c = Q_DIM
    o_qi = o_c + KV_LATENT
    o_ki = o_qi + IDX_Q_DIM
    o_xr = o_ki + IDX_DIM + IDX_HEADS
    wb = w_in.astype(BF16)
    w_qq = jnp.concatenate([wb[:, :o_c], wb[:, o_qi:o_ki]], axis=1)
    w_kw = jnp.pad(wb[:, o_ki:o_xr], ((0, 0), (0, LANES - IDX_DIM - IDX_HEADS)))
    qq = matmul(h0b, w_qq, BF16, 1024, 512, "proj_q")
    c = matmul_rms(h0b, wb[:, o_c:o_qi], kv_norm_g)
    kw = matmul(h0b, w_kw, F32, 1024, LANES, "proj_idx_k")
    zz = matmul(h0b, wb[:, o_xr:], F32, 1024, 512, "proj_lru_gates")
    attn = dsa_attention(qq, kw, c, w_uk, w_uv)
    lru = rg_lru(zz, conv_w, conv_b, w_rg_a, b_rg_a, w_rg_x, b_rg_x, rg_lambda)
    merged = merge_branches(attn, lru, w_branch_a.astype(BF16), w_branch_b.astype(BF16), zz)
    return out_proj_ln(merged, w_out.astype(BF16), h0f, ln1_g, ln1_b)


def _moe(h1, w_router, router_bias, w_gate_e, w_up_e, w_down_e, w_gate_s, w_up_s, w_down_s, ln2_g, ln2_b):
    T = h1.shape[0]
    eidx, gate, pos, cnt = router(h1, w_router, router_bias)
    counts = cnt[0].astype(I32)
    starts = jnp.cumsum(counts) - counts
    eidx = eidx.reshape(T * TOP_K)
    pos = pos.reshape(T * TOP_K)
    xs = dispatch(starts, eidx, pos, h1)
    ys = expert_ffn(visit_tables(counts, T * TOP_K), xs, w_gate_e, w_up_e, w_down_e)
    return combine(starts, eidx, pos, gate, h1, w_gate_s, w_up_s, w_down_s, ln2_g, ln2_b, ys)


def kernel(x, ln_in_g, ln_in_b, w_in, kv_norm_g, w_uk, w_uv, conv_w, conv_b, w_rg_a, b_rg_a, w_rg_x, b_rg_x,
           rg_lambda, w_branch_a, w_branch_b, w_out, ln1_g, ln1_b, w_router, router_bias, w_gate_e, w_up_e,
           w_down_e, w_gate_s, w_up_s, w_down_s, ln2_g, ln2_b):
    B, T, D = x.shape
    assert B == 1 and D == D_MODEL and w_in.shape[0] == DEPTH
    hf, hb = ln_in(x.reshape(T, D), ln_in_g, ln_in_b)
    for l in range(DEPTH):
        h1 = _mixer(hf, hb, w_in[l], kv_norm_g[l], w_uk[l], w_uv[l], conv_w[l], conv_b[l], w_rg_a[l], b_rg_a[l],
                    w_rg_x[l], b_rg_x[l], rg_lambda[l], w_branch_a[l], w_branch_b[l], w_out[l], ln1_g[l], ln1_b[l])
        hf = _moe(h1, w_router[l], router_bias[l], w_gate_e[l], w_up_e[l], w_down_e[l], w_gate_s[l], w_up_s[l],
                  w_down_s[l], ln2_g[l], ln2_b[l])
        hb = hf.astype(BF16)
    return hf.reshape(B, T, D)
```

```python
import functools

import jax
import jax.numpy as jnp
from jax import lax
from jax.experimental import pallas as pl
from jax.experimental.pallas import tpu as pltpu

F32 = jnp.float32
BF16 = jnp.bfloat16
I32 = jnp.int32

D_MODEL = 2048
CHUNK = 64
N_HEADS = 16
HEAD_DIM = 128
KV_LATENT = 256
IDX_HEADS = 16
IDX_DIM = 64
TOPK_KEYS_MAX = 256
ATTN_SCALE = HEAD_DIM ** -0.5
LRU_WIDTH = 2048
LRU_BLOCKS = 16
LRU_BLOCK_DIM = LRU_WIDTH // LRU_BLOCKS
CONV_WIDTH = 4
RG_C = 8.0
N_EXPERTS = 64
TOP_K = 8
N_GROUPS = 8
TOPK_GROUPS = 4
EXPERT_FF = 512
SHARED_FF = 512
ROUTE_SCALE = 2.5
LN_EPS = 1e-5
RMS_EPS = 1e-6
Q_DIM = N_HEADS * HEAD_DIM
IDX_Q_DIM = IDX_HEADS * IDX_DIM
DEPTH = 1
ALPHA = (2.0 * DEPTH) ** 0.25
LOG2_E = 1.4426950408889634

VMEM_LIMIT_V7X = 56 * 1024 * 1024
LANES = 128

ATTN_QB = 128
ATTN_TK = 512
EXPERT_TILE = 128
ROW_TILE = 128

NT_DIMS = (((1,), (1,)), ((), ()))


def _cparams(n_axes=1, vmem=VMEM_LIMIT_V7X):
    return pltpu.CompilerParams(dimension_semantics=("arbitrary",) * n_axes, vmem_limit_bytes=vmem)


def _layer_norm_rows(x, g, b):
    mu = jnp.mean(x, axis=-1, keepdims=True)
    xc = x - mu
    var = jnp.mean(xc * xc, axis=-1, keepdims=True)
    return xc * lax.rsqrt(var + LN_EPS) * g + b


def _sigmoid(x):
    return 1.0 / (1.0 + jnp.exp(-x))


def _ln_in_kernel(x_ref, g_ref, b_ref, hf_ref, hb_ref):
    y = _layer_norm_rows(x_ref[...], g_ref[...], b_ref[...])
    hf_ref[...] = y
    hb_ref[...] = y.astype(BF16)


def ln_in(x, g, b, tm=512):
    T, D = x.shape
    row = pl.BlockSpec((tm, D), lambda i: (i, 0))
    vec = pl.BlockSpec((1, D), lambda i: (0, 0))
    return pl.pallas_call(
        _ln_in_kernel, grid=(T // tm,), in_specs=[row, vec, vec], out_specs=[row, row],
        out_shape=[jax.ShapeDtypeStruct((T, D), F32), jax.ShapeDtypeStruct((T, D), BF16)],
        compiler_params=_cparams(), name="ln_in")(x, g.reshape(1, D), b.reshape(1, D))


def _mm_kernel(x_ref, w_ref, o_ref):
    o_ref[...] = jnp.dot(x_ref[...], w_ref[...], preferred_element_type=F32).astype(o_ref.dtype)


def matmul(x, w, col0, N, out_dtype, tm, tn, name):
    M, K = x.shape
    assert col0 % tn == 0 and N % tn == 0
    off = col0 // tn
    return pl.pallas_call(
        _mm_kernel, grid=(M // tm, N // tn),
        in_specs=[pl.BlockSpec((tm, K), lambda i, j: (i, 0)), pl.BlockSpec((K, tn), lambda i, j: (0, off + j))],
        out_specs=pl.BlockSpec((tm, tn), lambda i, j: (i, j)),
        out_shape=jax.ShapeDtypeStruct((M, N), out_dtype),
        compiler_params=_cparams(2), name=name)(x, w)


def _mm_rms_kernel(x_ref, w_ref, g_ref, o_ref):
    c = jnp.dot(x_ref[...], w_ref[...], preferred_element_type=F32)
    y = c * lax.rsqrt(jnp.mean(c * c, axis=-1, keepdims=True) + RMS_EPS) * g_ref[...]
    o_ref[...] = y.astype(o_ref.dtype)


def matmul_rms(x, w, col0, N, g, tm=1024):
    M, K = x.shape
    assert col0 % N == 0
    return pl.pallas_call(
        _mm_rms_kernel, grid=(M // tm,),
        in_specs=[pl.BlockSpec((tm, K), lambda i: (i, 0)), pl.BlockSpec((K, N), lambda i: (0, col0 // N)),
                  pl.BlockSpec((1, N), lambda i: (0, 0))],
        out_specs=pl.BlockSpec((tm, N), lambda i: (i, 0)),
        out_shape=jax.ShapeDtypeStruct((M, N), BF16),
        compiler_params=_cparams(), name="proj_kv_rms")(x, w, g.reshape(1, N))


def _attn_kernel(q_ref, qi_ref, wt_ref, kidx_ref, c_ref, ct_ref, wukt_ref, wuv_ref, o_ref,
                 sc_ref, big_ref, bias_ref, acc_ref, m_ref, l_ref, qlt_ref, qis_ref, thr_ref, j_ref,
                 *, QB, TK, NSEL):
    H = N_HEADS
    NP = H // 2
    b = pl.program_id(0)
    lmax = (b + 1) * QB
    nk = lax.div(lmax + (TK - 1), TK)
    ksel = float(NSEL)
    neg_inf = float("-inf")
    fmax = float(jnp.finfo(jnp.float32).max)

    tpos = b * QB + lax.broadcasted_iota(I32, (1, QB), 1)
    limit = (lax.shift_right_logical(tpos, 6) + 1) * CHUNK
    limit_f = limit.astype(F32)

    for h in range(H):
        qis_ref[h * QB:(h + 1) * QB, :] = qi_ref[:, h * IDX_DIM:(h + 1) * IDX_DIM]
        qlat_t = lax.dot_general(wukt_ref[h], q_ref[:, h * HEAD_DIM:(h + 1) * HEAD_DIM], NT_DIMS,
                                 preferred_element_type=F32)
        qlt_ref[h // 2, :, (h % 2) * QB:(h % 2 + 1) * QB] = (qlat_t * (ATTN_SCALE * LOG2_E)).astype(BF16)
    wscale = (IDX_HEADS ** -0.5) * (IDX_DIM ** -0.5)
    wrows = [wt_ref[h:h + 1, :] * wscale for h in range(H)]

    RC = 128

    def p1(kt, carry):
        k0 = pl.multiple_of(kt * TK, TK)
        big_ref[...] = lax.dot_general(kidx_ref[pl.ds(k0, TK), :], qis_ref[...], NT_DIMS,
                                       preferred_element_type=F32)
        for r in range(TK // RC):
            rows = slice(r * RC, (r + 1) * RC)
            acc = jnp.zeros((RC, QB), F32)
            for h in range(H):
                acc = acc + wrows[h] * jnp.maximum(big_ref[rows, h * QB:(h + 1) * QB], 0.0)
            key = k0 + r * RC + lax.broadcasted_iota(I32, (RC, QB), 0)
            sc_ref[kt, rows, :] = jnp.where(key < limit, acc, neg_inf)
        return carry

    lax.fori_loop(0, nk, p1, 0)

    CR = 32

    def count_ge(t):
        tb = jnp.broadcast_to(t, (CR, QB))

        def body(kt, cnt):
            hit = jnp.where(sc_ref[kt].reshape(TK // CR, CR, QB) >= tb, 1.0, 0.0)
            return cnt + jnp.sum(hit, axis=0)

        cnt = lax.fori_loop(0, nk, body, jnp.zeros((CR, QB), F32))
        return jnp.sum(cnt, axis=0, keepdims=True)

    def minmax_body(kt, carry):
        mn, mx = carry
        x = sc_ref[kt]
        mx = jnp.maximum(mx, jnp.max(x, axis=0, keepdims=True))
        mn = jnp.minimum(mn, jnp.min(jnp.where(x == neg_inf, fmax, x), axis=0, keepdims=True))
        return mn, mx

    mn, mx = lax.fori_loop(0, nk, minmax_body,
                           (jnp.full((1, QB), fmax, F32), jnp.full((1, QB), neg_inf, F32)))

    c_top = count_ge(mx)
    one = jnp.ones((1, QB), F32)
    zero = jnp.zeros((1, QB), F32)
    all_sel = jnp.where(limit_f <= ksel, one, zero)
    top_tie = (1.0 - all_sel) * jnp.where(c_top >= ksel, one, zero)
    lo0 = jnp.where(top_tie > 0, mx, mn)
    c_hi0 = jnp.where(top_tie > 0, zero, c_top)
    thr0 = jnp.where(all_sel > 0, -fmax, mx)
    tie0 = top_tie * jnp.where(c_top > ksel, one, zero)
    done0 = jnp.maximum(all_sel, top_tie)

    def bis_cond(st):
        return jnp.logical_and(st[0] > 0.5, st[1] < 400)

    def bis_body(st):
        _, it, lo, hi, c_hi, thr, tie, done = st
        n_act = jnp.max(1.0 - done)
        act = done < 0.5
        mid = 0.5 * lo + 0.5 * hi
        adjacent = jnp.logical_or(mid <= lo, mid >= hi)
        c = count_ge(mid)
        ge = c >= ksel
        exact = c == ksel
        stop = jnp.logical_or(exact, adjacent)
        fin_tie = jnp.logical_and(act, jnp.logical_and(adjacent, jnp.logical_not(exact)))
        thr = jnp.where(jnp.logical_and(act, exact), mid, jnp.where(fin_tie, lo, thr))
        tie = jnp.where(fin_tie, one, tie)
        upd = jnp.logical_and(act, jnp.logical_not(stop))
        up_lo = jnp.logical_and(upd, ge)
        up_hi = jnp.logical_and(upd, jnp.logical_not(ge))
        lo = jnp.where(up_lo, mid, lo)
        hi = jnp.where(up_hi, mid, hi)
        c_hi = jnp.where(up_hi, c, c_hi)
        done = jnp.where(jnp.logical_and(act, stop), one, done)
        return n_act, it + 1, lo, hi, c_hi, thr, tie, done

    st = lax.while_loop(bis_cond, bis_body,
                        (jnp.max(1.0 - done0), jnp.int32(0), lo0, mx, c_hi0, thr0, tie0, done0))
    _, _, _, _, c_hi, thr, tie, _ = st
    thr_ref[...] = thr
    big_j = float(2 ** 30)
    j_ref[...] = jnp.full((1, QB), big_j, F32)

    @pl.when(jnp.max(tie) > 0.5)
    def _():
        need = ksel - c_hi

        def count_tie_below(jb):
            def body(kt, cnt):
                keyf = (kt * TK + lax.broadcasted_iota(I32, (TK, QB), 0)).astype(F32)
                hit = jnp.logical_and(sc_ref[kt] == thr, keyf < jb)
                return cnt + jnp.sum(jnp.where(hit, 1.0, 0.0), axis=0, keepdims=True)

            return lax.fori_loop(0, nk, body, jnp.zeros((1, QB), F32))

        def jbody(_, carry):
            jlo, jhi = carry
            jm = jnp.floor((jlo + jhi) * 0.5)
            ok = count_tie_below(jm) >= need
            return jnp.where(ok, jlo, jm), jnp.where(ok, jm, jhi)

        n_steps = max(1, int(sc_ref.shape[0] * TK).bit_length())
        _, jhi = lax.fori_loop(0, n_steps, jbody, (zero, zero + (nk * TK).astype(F32)))
        j_ref[...] = jnp.where(tie > 0.5, jhi, big_j)

    m_ref[...] = jnp.full(m_ref.shape, -1e30, F32)
    l_ref[...] = jnp.zeros(l_ref.shape, F32)
    acc_ref[...] = jnp.zeros(acc_ref.shape, F32)

    def p3(kt, carry):
        k0 = pl.multiple_of(kt * TK, TK)
        ck = c_ref[pl.ds(k0, TK), :]
        ckt = ct_ref[kt]
        x = sc_ref[kt]
        keyf = (k0 + lax.broadcasted_iota(I32, (TK, QB), 0)).astype(F32)
        t = thr_ref[...]
        sel = jnp.logical_or(x > t, jnp.logical_and(x == t, keyf < j_ref[...]))
        bias_ref[...] = jnp.where(sel, 0.0, neg_inf)
        for pr in range(NP):
            s = jnp.dot(ck, qlt_ref[pr], preferred_element_type=F32)
            bias = bias_ref[...]
            s = jnp.concatenate([s[:, :QB] + bias, s[:, QB:] + bias], axis=1)
            m_old = m_ref[pr]
            m_new = jnp.maximum(m_old, jnp.max(s, axis=0, keepdims=True))
            p = jnp.exp2(s - m_new)
            alpha = jnp.exp2(m_old - m_new)
            l_ref[pr] = alpha * l_ref[pr] + jnp.sum(p, axis=0, keepdims=True)
            m_ref[pr] = m_new
            pv = jnp.dot(ckt, p.astype(BF16), preferred_element_type=F32)
            acc_ref[pr] = alpha * acc_ref[pr] + pv
        return carry

    lax.fori_loop(0, nk, p3, 0)

    for h in range(H):
        cols = slice((h % 2) * QB, (h % 2 + 1) * QB)
        o_lat_t = acc_ref[h // 2, :, cols] / l_ref[h // 2, :, cols]
        o_ref[:, h * HEAD_DIM:(h + 1) * HEAD_DIM] = jnp.dot(
            o_lat_t.T.astype(BF16), wuv_ref[h], preferred_element_type=F32).astype(o_ref.dtype)


def dsa_attention(qq, kw, c, w_uk, w_uv, qb=ATTN_QB, tk=ATTN_TK):
    T = c.shape[0]
    tk = min(tk, T)
    nsel = min(TOPK_KEYS_MAX, T // 4)
    assert qb % CHUNK == 0 and T % tk == 0 and T % qb == 0 and tk % qb == 0
    H = N_HEADS
    kidx = kw[:, :IDX_DIM].astype(BF16)
    w_t = kw[:, IDX_DIM:IDX_DIM + IDX_HEADS].T
    c_t = c.reshape(T // tk, tk, KV_LATENT).transpose(0, 2, 1)
    kernel = functools.partial(_attn_kernel, QB=qb, TK=tk, NSEL=nsel)
    full = lambda shape: pl.BlockSpec(shape, lambda i: (0,) * len(shape))
    return pl.pallas_call(
        kernel, grid=(T // qb,),
        in_specs=[pl.BlockSpec((qb, Q_DIM), lambda i: (i, 0)),
                  pl.BlockSpec((qb, IDX_Q_DIM), lambda i: (i, Q_DIM // IDX_Q_DIM)),
                  pl.BlockSpec((H, qb), lambda i: (0, i)),
                  full((T, IDX_DIM)), full((T, KV_LATENT)), full((T // tk, KV_LATENT, tk)),
                  full((H, KV_LATENT, HEAD_DIM)), full((H, KV_LATENT, HEAD_DIM))],
        out_specs=pl.BlockSpec((qb, Q_DIM), lambda i: (i, 0)),
        out_shape=jax.ShapeDtypeStruct((T, Q_DIM), BF16),
        scratch_shapes=[pltpu.VMEM((T // tk, tk, qb), F32),
                        pltpu.VMEM((tk, H * qb), F32),
                        pltpu.VMEM((tk, qb), F32),
                        pltpu.VMEM((H // 2, KV_LATENT, 2 * qb), F32),
                        pltpu.VMEM((H // 2, 1, 2 * qb), F32), pltpu.VMEM((H // 2, 1, 2 * qb), F32),
                        pltpu.VMEM((H // 2, KV_LATENT, 2 * qb), BF16),
                        pltpu.VMEM((H * qb, IDX_DIM), BF16),
                        pltpu.VMEM((1, qb), F32), pltpu.VMEM((1, qb), F32)],
        compiler_params=_cparams(), name="dsa_attention")(
            qq, qq, w_t, kidx, c, c_t, w_uk.transpose(0, 2, 1).astype(BF16), w_uv.astype(BF16))


def _gelu_tanh(x):
    return 0.5 * x * (1.0 + jnp.tanh(0.7978845608028654 * (x + 0.044715 * (x * x * x))))


def _lru_kernel(xr_ref, yg_ref, cw_ref, cb_ref, wa_ref, ba_ref, wx_ref, bx_ref, lam_ref, o_ref,
                ext_ref, a_ref, u_ref, hs_ref, h_ref, *, TT):
    i = pl.program_id(0)
    C = LRU_WIDTH

    @pl.when(i == 0)
    def _():
        ext_ref[0:8, :] = jnp.zeros((8, C), F32)
        h_ref[...] = jnp.zeros((1, C), F32)

    ext_ref[8:8 + TT, :] = xr_ref[...]
    xc = jnp.zeros((TT, C), F32) + cb_ref[...]
    for j in range(CONV_WIDTH):
        off = 8 - (CONV_WIDTH - 1) + j
        xc = xc + cw_ref[j:j + 1, :] * ext_ref[off:off + TT, :]
    ext_ref[0:8, :] = ext_ref[TT:TT + 8, :]

    lam = lam_ref[...]
    softplus_neg = jnp.maximum(-lam, 0.0) + jnp.log1p(jnp.exp(-jnp.abs(lam)))
    xcb = xc.astype(BF16)
    for n in range(LRU_BLOCKS):
        cols = slice(n * LRU_BLOCK_DIM, (n + 1) * LRU_BLOCK_DIM)
        xb = xcb[:, cols]
        r = _sigmoid(jnp.dot(xb, wa_ref[n], preferred_element_type=F32) + ba_ref[:, cols])
        g = _sigmoid(jnp.dot(xb, wx_ref[n], preferred_element_type=F32) + bx_ref[:, cols])
        log_a = (-RG_C) * r * softplus_neg[:, cols]
        a_ref[:, cols] = jnp.exp(log_a)
        th = jnp.tanh(log_a)
        u_ref[:, cols] = jnp.sqrt((-2.0 * th) / (1.0 - th)) * (g * xc[:, cols])

    def step(t, h):
        h = a_ref[pl.ds(t, 1), :] * h + u_ref[pl.ds(t, 1), :]
        hs_ref[pl.ds(t, 1), :] = h
        return h

    h_ref[...] = lax.fori_loop(0, TT, step, h_ref[...], unroll=8)
    o_ref[...] = (hs_ref[...] * _gelu_tanh(yg_ref[...])).astype(o_ref.dtype)


def rg_lru(zz, conv_w, conv_b, w_a, b_a, w_x, b_x, lam, tt=256):
    T = zz.shape[0]
    tt = min(tt, T)
    C = LRU_WIDTH
    vec = pl.BlockSpec((1, C), lambda i: (0, 0))
    blk = pl.BlockSpec((LRU_BLOCKS, LRU_BLOCK_DIM, LRU_BLOCK_DIM), lambda i: (0, 0, 0))
    return pl.pallas_call(
        functools.partial(_lru_kernel, TT=tt), grid=(T // tt,),
        in_specs=[pl.BlockSpec((tt, C), lambda i: (i, 0)), pl.BlockSpec((tt, C), lambda i: (i, 1)),
                  pl.BlockSpec((CONV_WIDTH, C), lambda i: (0, 0)), vec, blk, vec, blk, vec, vec],
        out_specs=pl.BlockSpec((tt, C), lambda i: (i, 0)),
        out_shape=jax.ShapeDtypeStruct((T, C), BF16),
        scratch_shapes=[pltpu.VMEM((tt + 8, C), F32), pltpu.VMEM((tt, C), F32), pltpu.VMEM((tt, C), F32),
                        pltpu.VMEM((tt, C), F32), pltpu.VMEM((1, C), F32)],
        compiler_params=_cparams(), name="rg_lru")(
            zz, zz, conv_w, conv_b.reshape(1, C), w_a.astype(BF16), b_a.reshape(1, C),
            w_x.astype(BF16), b_x.reshape(1, C), lam.reshape(1, C))


def _merge_kernel(a_ref, l_ref, wa_ref, wb_ref, ga_ref, gb_ref, o_ref):
    ya = jnp.dot(a_ref[...], wa_ref[...], preferred_element_type=F32)
    yb = jnp.dot(l_ref[...], wb_ref[...], preferred_element_type=F32)
    o_ref[...] = (_sigmoid(ga_ref[...]) * ya + _sigmoid(gb_ref[...]) * yb).astype(o_ref.dtype)


def merge_branches(attn, lru, w_a, w_b, zz, tm=512, tn=512):
    T, D = attn.shape
    nb = D // tn
    return pl.pallas_call(
        _merge_kernel, grid=(T // tm, nb),
        in_specs=[pl.BlockSpec((tm, D), lambda i, j: (i, 0)), pl.BlockSpec((tm, D), lambda i, j: (i, 0)),
                  pl.BlockSpec((D, tn), lambda i, j: (0, j)), pl.BlockSpec((D, tn), lambda i, j: (0, j)),
                  pl.BlockSpec((tm, tn), lambda i, j: (i, 2 * nb + j)),
                  pl.BlockSpec((tm, tn), lambda i, j: (i, 3 * nb + j))],
        out_specs=pl.BlockSpec((tm, tn), lambda i, j: (i, j)),
        out_shape=jax.ShapeDtypeStruct((T, D), BF16),
        compiler_params=_cparams(2), name="merge_branches")(attn, lru, w_a, w_b, zz, zz)


def _post1_kernel(m_ref, w_ref, h_ref, g_ref, b_ref, o_ref):
    y = ALPHA * h_ref[...] + jnp.dot(m_ref[...], w_ref[...], preferred_element_type=F32)
    o_ref[...] = _layer_norm_rows(y, g_ref[...], b_ref[...])


def out_proj_ln(merged, w_out, h0, g, b, tm=512):
    T, D = merged.shape
    row = lambda dt: pl.BlockSpec((tm, D), lambda i: (i, 0))
    vec = pl.BlockSpec((1, D), lambda i: (0, 0))
    return pl.pallas_call(
        _post1_kernel, grid=(T // tm,),
        in_specs=[row(BF16), pl.BlockSpec((D, D), lambda i: (0, 0)), row(F32), vec, vec],
        out_specs=row(F32), out_shape=jax.ShapeDtypeStruct((T, D), F32),
        compiler_params=_cparams(), name="out_proj_ln1")(merged, w_out, h0, g.reshape(1, D), b.reshape(1, D))


def _router_kernel(h_ref, wr_ref, rb_ref, eidx_ref, gate_ref, pos_ref, cnt_ref, carry_ref, *, TM):
    i = pl.program_id(0)
    E = N_EXPERTS
    per_group = E // N_GROUPS
    neg_inf = float("-inf")

    @pl.when(i == 0)
    def _():
        carry_ref[...] = jnp.zeros((1, E), F32)

    h = h_ref[...]
    hh = h.astype(BF16)
    hl = (h - hh.astype(F32)).astype(BF16)
    w = wr_ref[...]
    wh = w.astype(BF16)
    wl = (w - wh.astype(F32)).astype(BF16)
    logits = (jnp.dot(hh, wh, preferred_element_type=F32)
              + (jnp.dot(hh, wl, preferred_element_type=F32) + jnp.dot(hl, wh, preferred_element_type=F32)))
    scores = _sigmoid(logits)
    biased = scores + rb_ref[...]
    lane = lax.broadcasted_iota(I32, (TM, E), 1)

    def first_argmax(v):
        m = jnp.max(v, axis=1, keepdims=True)
        idx = jnp.min(jnp.where(v == m, lane, E), axis=1, keepdims=True)
        return m, idx

    gscore = []
    for g in range(N_GROUPS):
        in_g = jnp.logical_and(lane >= g * per_group, lane < (g + 1) * per_group)
        vg = jnp.where(in_g, biased, neg_inf)
        m1, i1 = first_argmax(vg)
        m2 = jnp.max(jnp.where(lane == i1, neg_inf, vg), axis=1, keepdims=True)
        gscore.append(m1 + m2)
    cand = jnp.full((TM, E), neg_inf, F32)
    for g in range(N_GROUPS):
        rank = jnp.zeros((TM, 1), F32)
        for g2 in range(N_GROUPS):
            if g2 == g:
                continue
            ahead = (gscore[g2] >= gscore[g]) if g2 < g else (gscore[g2] > gscore[g])
            rank = rank + jnp.where(ahead, 1.0, 0.0)
        in_g = jnp.logical_and(lane >= g * per_group, lane < (g + 1) * per_group)
        keep = jnp.logical_and(in_g, rank < float(TOPK_GROUPS))
        cand = jnp.where(keep, biased, cand)

    lane_k = lax.broadcasted_iota(I32, (TM, TOP_K), 1)
    sel = jnp.zeros((TM, E), F32)
    onehots, gates = [], []
    eidx = jnp.zeros((TM, TOP_K), I32)
    for k in range(TOP_K):
        _, ik = first_argmax(cand)
        oh = lane == ik
        onehots.append(oh)
        gates.append(jnp.sum(jnp.where(oh, scores, 0.0), axis=1, keepdims=True))
        cand = jnp.where(oh, neg_inf, cand)
        sel = sel + jnp.where(oh, 1.0, 0.0)
        eidx = jnp.where(lane_k == k, ik, eidx)
    gsum = gates[0]
    for k in range(1, TOP_K):
        gsum = gsum + gates[k]

    r_i = lax.broadcasted_iota(I32, (TM, TM), 0)
    c_i = lax.broadcasted_iota(I32, (TM, TM), 1)
    tri = jnp.where(c_i < r_i, 1.0, 0.0).astype(BF16)
    prefix = jnp.dot(tri, sel.astype(BF16), preferred_element_type=F32) + carry_ref[...]
    gate = jnp.zeros((TM, TOP_K), F32)
    pos = jnp.zeros((TM, TOP_K), F32)
    for k in range(TOP_K):
        gk = gates[k] / gsum * ROUTE_SCALE
        pk = jnp.sum(jnp.where(onehots[k], prefix, 0.0), axis=1, keepdims=True)
        gate = jnp.where(lane_k == k, gk, gate)
        pos = jnp.where(lane_k == k, pk, pos)
    eidx_ref[...] = eidx
    gate_ref[...] = gate
    pos_ref[...] = pos.astype(I32)
    carry_ref[...] = carry_ref[...] + jnp.sum(sel, axis=0, keepdims=True)
    cnt_ref[...] = carry_ref[...]


def router(h1, w_router, router_bias, tm=512):
    T, D = h1.shape
    tm = min(tm, T)
    E = N_EXPERTS
    out8 = pl.BlockSpec((tm, TOP_K), lambda i: (i, 0))
    return pl.pallas_call(
        functools.partial(_router_kernel, TM=tm), grid=(T // tm,),
        in_specs=[pl.BlockSpec((tm, D), lambda i: (i, 0)), pl.BlockSpec((D, E), lambda i: (0, 0)),
                  pl.BlockSpec((1, E), lambda i: (0, 0))],
        out_specs=[out8, out8, out8, pl.BlockSpec((1, E), lambda i: (0, 0))],
        out_shape=[jax.ShapeDtypeStruct((T, TOP_K), I32), jax.ShapeDtypeStruct((T, TOP_K), F32),
                   jax.ShapeDtypeStruct((T, TOP_K), I32), jax.ShapeDtypeStruct((1, E), F32)],
        scratch_shapes=[pltpu.VMEM((1, E), F32)],
        compiler_params=_cparams(), name="router")(h1, w_router, router_bias.reshape(1, E))


def _row_copy(src, src_row, dst, dst_row, sem):
    return pltpu.make_async_copy(src.at[pl.ds(src_row, 1)], dst.at[pl.ds(dst_row, 1)], sem)


def _dest_row(starts_ref, eidx_ref, pos_ref, i):
    return starts_ref[eidx_ref[i]] + pos_ref[i]


def _dispatch_kernel(starts_ref, eidx_ref, pos_ref, h_ref, xs_hbm, sem, *, TM):
    def start(n, carry):
        for k in range(TOP_K):
            d = _dest_row(starts_ref, eidx_ref, pos_ref, n * TOP_K + k)
            _row_copy(h_ref, n, xs_hbm, d, sem).start(priority=k % 2)
        return carry

    def wait(n, carry):
        for k in range(TOP_K):
            _row_copy(h_ref, 0, xs_hbm, 0, sem).wait()
        return carry

    lax.fori_loop(0, TM, start, 0)
    lax.fori_loop(0, TM, wait, 0)


def _route_specs(tm):
    idx = pl.BlockSpec((tm * TOP_K,), lambda i: (i,), memory_space=pltpu.SMEM)
    return [pl.BlockSpec((N_EXPERTS,), lambda i: (0,), memory_space=pltpu.SMEM), idx, idx]


def dispatch(starts, eidx, pos, h1, tm=ROW_TILE):
    T, D = h1.shape
    return pl.pallas_call(
        functools.partial(_dispatch_kernel, TM=tm), grid=(T // tm,),
        in_specs=_route_specs(tm) + [pl.BlockSpec((tm, D), lambda i: (i, 0))],
        out_specs=pl.BlockSpec(memory_space=pl.ANY),
        out_shape=jax.ShapeDtypeStruct((T * TOP_K, D), h1.dtype),
        scratch_shapes=[pltpu.SemaphoreType.DMA(())],
        compiler_params=_cparams(), name="moe_dispatch")(starts, eidx, pos, h1)


VISIT_FIRST_OF_TILE = 1
VISIT_FIRST_OF_EXPERT = 2
VISIT_ACTIVE = 4
VISIT_HAS_NEXT_EXPERT = 8


def _expert_kernel(vt_ref, ve_ref, vlo_ref, vhi_ref, vfl_ref, ven_ref, vsl_ref,
                   x_ref, wg_hbm, wu_hbm, wd_hbm, o_ref,
                   wgf_ref, wuf_ref, wdf_ref, wgb_ref, wub_ref, wdb_ref, wsem, *, TE):
    v = pl.program_id(0)
    flags = vfl_ref[v]
    slot = vsl_ref[v]

    def weight_copies(expert, s):
        return (pltpu.make_async_copy(wg_hbm.at[expert], wgf_ref.at[s], wsem.at[0, s]),
                pltpu.make_async_copy(wu_hbm.at[expert], wuf_ref.at[s], wsem.at[1, s]),
                pltpu.make_async_copy(wd_hbm.at[expert], wdf_ref.at[s], wsem.at[2, s]))

    @pl.when(v == 0)
    def _():
        for cp in weight_copies(ve_ref[0], 0):
            cp.start()

    @pl.when((flags & VISIT_FIRST_OF_EXPERT) != 0)
    def _():
        for cp in weight_copies(ve_ref[v], slot):
            cp.wait()

        @pl.when((flags & VISIT_HAS_NEXT_EXPERT) != 0)
        def _():
            for cp in weight_copies(ven_ref[v], 1 - slot):
                cp.start()

        wgb_ref[...] = wgf_ref[slot].astype(BF16)
        wub_ref[...] = wuf_ref[slot].astype(BF16)
        wdb_ref[...] = wdf_ref[slot].astype(BF16)

    @pl.when((flags & VISIT_ACTIVE) != 0)
    def _():
        x = x_ref[...].astype(BF16)
        g = jnp.dot(x, wgb_ref[...], preferred_element_type=F32)
        u = jnp.dot(x, wub_ref[...], preferred_element_type=F32)
        rows = vt_ref[v] * TE + lax.broadcasted_iota(I32, (TE, 1), 0)
        mine = jnp.logical_and(rows >= vlo_ref[v], rows < vhi_ref[v])
        mid = jnp.where(mine, (g * _sigmoid(g)) * u, 0.0).astype(BF16)
        y = jnp.dot(mid, wdb_ref[...], preferred_element_type=F32)

        @pl.when((flags & VISIT_FIRST_OF_TILE) != 0)
        def _():
            o_ref[...] = y

        @pl.when((flags & VISIT_FIRST_OF_TILE) == 0)
        def _():
            o_ref[...] = o_ref[...] + y


def expert_ffn(tables, xs, w_gate_e, w_up_e, w_down_e, te=EXPERT_TILE):
    R, D = xs.shape
    F = EXPERT_FF
    nv = tables[0].shape[0]
    row_map = lambda v, vt, *_: (vt[v], 0)
    hbm = pl.BlockSpec(memory_space=pl.ANY)
    grid_spec = pltpu.PrefetchScalarGridSpec(
        num_scalar_prefetch=len(tables), grid=(nv,),
        in_specs=[pl.BlockSpec((te, D), row_map), hbm, hbm, hbm],
        out_specs=pl.BlockSpec((te, D), row_map),
        scratch_shapes=[pltpu.VMEM((2, D, F), F32), pltpu.VMEM((2, D, F), F32), pltpu.VMEM((2, F, D), F32),
                        pltpu.VMEM((D, F), BF16), pltpu.VMEM((D, F), BF16), pltpu.VMEM((F, D), BF16),
                        pltpu.SemaphoreType.DMA((3, 2))])
    return pl.pallas_call(
        functools.partial(_expert_kernel, TE=te), grid_spec=grid_spec,
        out_shape=jax.ShapeDtypeStruct((R, D), F32),
        compiler_params=_cparams(), name="expert_ffn")(*tables, xs, w_gate_e, w_up_e, w_down_e)


def visit_tables(counts, n_rows, te=EXPERT_TILE):
    E = counts.shape[0]
    n_tiles = n_rows // te
    nv = n_tiles + E
    ends = jnp.cumsum(counts)
    starts = ends - counts
    first_tile = starts // te
    last_tile = jnp.maximum(ends - 1, 0) // te
    nvis = jnp.where(counts > 0, last_tile - first_tile + 1, 0)
    vis_end = jnp.cumsum(nvis)
    vis_start = vis_end - nvis
    total = vis_end[-1]
    ids = jnp.arange(E, dtype=I32)[None, :]
    v = jnp.arange(nv, dtype=I32)
    vc = jnp.minimum(v, total - 1)
    e = jnp.sum((vis_end[None, :] <= vc[:, None]).astype(I32), axis=1)
    onehot = e[:, None] == ids
    pick = lambda table: jnp.sum(jnp.where(onehot, table[None, :], 0), axis=1)
    tile = (pick(first_tile - vis_start) + vc).astype(I32)
    lo = jnp.maximum(pick(starts), tile * te).astype(I32)
    hi = jnp.minimum(pick(ends), (tile + 1) * te).astype(I32)
    e_next = jnp.min(jnp.where(jnp.logical_and(ids > e[:, None], nvis[None, :] > 0), ids, E), axis=1)
    n_before = jnp.sum(jnp.where(jnp.logical_and(ids < e[:, None], nvis[None, :] > 0), 1, 0), axis=1)
    active = v < total
    first_of_tile = jnp.concatenate([jnp.ones((1,), bool), tile[1:] != tile[:-1]])
    first_of_expert = jnp.concatenate([jnp.ones((1,), bool), e[1:] != e[:-1]])
    flags = jnp.where(active,
                      VISIT_ACTIVE + jnp.where(first_of_tile, VISIT_FIRST_OF_TILE, 0)
                      + jnp.where(first_of_expert, VISIT_FIRST_OF_EXPERT, 0)
                      + jnp.where(e_next < E, VISIT_HAS_NEXT_EXPERT, 0), 0).astype(I32)
    slot = (n_before % 2).astype(I32)
    return tile, e.astype(I32), lo, hi, flags, jnp.minimum(e_next, E - 1).astype(I32), slot


def _combine_kernel(starts_ref, eidx_ref, pos_ref, gate_ref, h_ref, wgs_ref, wus_ref, wds_ref, g_ref, b_ref,
                    ys_hbm, o_ref, buf_ref, sem, *, TM):
    def start(n, carry):
        for k in range(TOP_K):
            d = _dest_row(starts_ref, eidx_ref, pos_ref, n * TOP_K + k)
            _row_copy(ys_hbm, d, buf_ref.at[k], n, sem).start(priority=k % 2)
        return carry

    def wait(n, carry):
        for k in range(TOP_K):
            _row_copy(ys_hbm, 0, buf_ref.at[k], 0, sem).wait()
        return carry

    lax.fori_loop(0, TM, start, 0)
    h = h_ref[...]
    hb = h.astype(BF16)
    g = jnp.dot(hb, wgs_ref[...], preferred_element_type=F32)
    u = jnp.dot(hb, wus_ref[...], preferred_element_type=F32)
    shared = jnp.dot(((g * _sigmoid(g)) * u).astype(BF16), wds_ref[...], preferred_element_type=F32)
    lax.fori_loop(0, TM, wait, 0)
    routed = jnp.zeros(h.shape, F32)
    for k in range(TOP_K):
        routed = routed + buf_ref[k] * gate_ref[:, k:k + 1]
    y = ALPHA * h + (routed + shared)
    o_ref[...] = _layer_norm_rows(y, g_ref[...], b_ref[...])


def combine(starts, eidx, pos, gate, h1, w_gate_s, w_up_s, w_down_s, g, b, ys, tm=ROW_TILE):
    T, D = h1.shape
    F = SHARED_FF
    row = pl.BlockSpec((tm, D), lambda i: (i, 0))
    vec = pl.BlockSpec((1, D), lambda i: (0, 0))
    return pl.pallas_call(
        functools.partial(_combine_kernel, TM=tm), grid=(T // tm,),
        in_specs=_route_specs(tm) + [
            pl.BlockSpec((tm, TOP_K), lambda i: (i, 0)), row,
            pl.BlockSpec((D, F), lambda i: (0, 0)), pl.BlockSpec((D, F), lambda i: (0, 0)),
            pl.BlockSpec((F, D), lambda i: (0, 0)), vec, vec,
            pl.BlockSpec(memory_space=pl.ANY)],
        out_specs=row, out_shape=jax.ShapeDtypeStruct((T, D), F32),
        scratch_shapes=[pltpu.VMEM((TOP_K, tm, D), F32), pltpu.SemaphoreType.DMA(())],
        compiler_params=_cparams(), name="moe_combine_ln2")(
            starts, eidx, pos, gate, h1, w_gate_s.astype(BF16), w_up_s.astype(BF16), w_down_s.astype(BF16),
            g.reshape(1, D), b.reshape(1, D), ys)


def _mixer(h0f, h0b, w_in, kv_norm_g, w_uk, w_uv, conv_w, conv_b, w_rg_a, b_rg_a, w_rg_x, b_rg_x, rg_lambda,
           w_branch_a, w_branch_b, w_out, ln1_g, ln1_b):
    o_c = Q_DIM
    o_qi = o_c + KV_LATENT
    o_ki = o_qi + IDX_Q_DIM
    o_xr = o_ki + IDX_DIM + IDX_HEADS
    n_gate = 4 * LRU_WIDTH
    w_kw = jnp.pad(w_in[:, o_ki:o_xr], ((0, 0), (0, LANES - IDX_DIM - IDX_HEADS)))
    w_all = jnp.concatenate([w_in[:, o_xr:], w_in[:, :o_c], w_in[:, o_qi:o_ki], w_in[:, o_c:o_qi], w_kw],
                            axis=1).astype(BF16)
    c_qq = n_gate
    c_c = c_qq + Q_DIM + IDX_Q_DIM
    c_kw = c_c + KV_LATENT
    zz = matmul(h0b, w_all, 0, n_gate, F32, 1024, 512, "proj_lru_gates")
    qq = matmul(h0b, w_all, c_qq, Q_DIM + IDX_Q_DIM, BF16, 1024, 512, "proj_q")
    c = matmul_rms(h0b, w_all, c_c, KV_LATENT, kv_norm_g)
    kw = matmul(h0b, w_all, c_kw, LANES, F32, 1024, LANES, "proj_idx_k")
    attn = dsa_attention(qq, kw, c, w_uk, w_uv)
    lru = rg_lru(zz, conv_w, conv_b, w_rg_a, b_rg_a, w_rg_x, b_rg_x, rg_lambda)
    merged = merge_branches(attn, lru, w_branch_a.astype(BF16), w_branch_b.astype(BF16), zz)
    return out_proj_ln(merged, w_out.astype(BF16), h0f, ln1_g, ln1_b)


def _moe(h1, w_router, router_bias, w_gate_e, w_up_e, w_down_e, w_gate_s, w_up_s, w_down_s, ln2_g, ln2_b):
    T = h1.shape[0]
    eidx, gate, pos, cnt = router(h1, w_router, router_bias)
    counts = cnt[0].astype(I32)
    starts = jnp.cumsum(counts) - counts
    eidx = eidx.reshape(T * TOP_K)
    pos = pos.reshape(T * TOP_K)
    xs = dispatch(starts, eidx, pos, h1)
    ys = expert_ffn(visit_tables(counts, T * TOP_K), xs, w_gate_e, w_up_e, w_down_e)
    return combine(starts, eidx, pos, gate, h1, w_gate_s, w_up_s, w_down_s, ln2_g, ln2_b, ys)


def kernel(x, ln_in_g, ln_in_b, w_in, kv_norm_g, w_uk, w_uv, conv_w, conv_b, w_rg_a, b_rg_a, w_rg_x, b_rg_x,
           rg_lambda, w_branch_a, w_branch_b, w_out, ln1_g, ln1_b, w_router, router_bias, w_gate_e, w_up_e,
           w_down_e, w_gate_s, w_up_s, w_down_s, ln2_g, ln2_b):
    B, T, D = x.shape
    assert B == 1 and D == D_MODEL and w_in.shape[0] == DEPTH
    hf, hb = ln_in(x.reshape(T, D), ln_in_g, ln_in_b)
    for l in range(DEPTH):
        h1 = _mixer(hf, hb, w_in[l], kv_norm_g[l], w_uk[l], w_uv[l], conv_w[l], conv_b[l], w_rg_a[l], b_rg_a[l],
                    w_rg_x[l], b_rg_x[l], rg_lambda[l], w_branch_a[l], w_branch_b[l], w_out[l], ln1_g[l], ln1_b[l])
        hf = _moe(h1, w_router[l], router_bias[l], w_gate_e[l], w_up_e[l], w_down_e[l], w_gate_s[l], w_up_s[l],
                  w_down_s[l], ln2_g[l], ln2_b[l])
        hb = hf.astype(BF16)
    return hf.reshape(B, T, D)
```

```python
import functools

import jax
import jax.numpy as jnp
from jax import lax
from jax.experimental import pallas as pl
from jax.experimental.pallas import tpu as pltpu

F32 = jnp.float32
BF16 = jnp.bfloat16
I32 = jnp.int32

D_MODEL = 2048
CHUNK = 64
N_HEADS = 16
HEAD_DIM = 128
KV_LATENT = 256
IDX_HEADS = 16
IDX_DIM = 64
TOPK_KEYS_MAX = 256
ATTN_SCALE = HEAD_DIM ** -0.5
LRU_WIDTH = 2048
LRU_BLOCKS = 16
LRU_BLOCK_DIM = LRU_WIDTH // LRU_BLOCKS
CONV_WIDTH = 4
RG_C = 8.0
N_EXPERTS = 64
TOP_K = 8
N_GROUPS = 8
TOPK_GROUPS = 4
EXPERT_FF = 512
SHARED_FF = 512
ROUTE_SCALE = 2.5
LN_EPS = 1e-5
RMS_EPS = 1e-6
Q_DIM = N_HEADS * HEAD_DIM
IDX_Q_DIM = IDX_HEADS * IDX_DIM
DEPTH = 1
ALPHA = (2.0 * DEPTH) ** 0.25
LOG2_E = 1.4426950408889634

VMEM_LIMIT_V7X = 56 * 1024 * 1024
LANES = 128
MXU_WIDTH_V7X = 256

ATTN_QB = 256
ATTN_TK = 512
EXPERT_TILE = 256
ROW_TILE = 128

NT_DIMS = (((1,), (1,)), ((), ()))


def _cparams(n_axes=1, vmem=VMEM_LIMIT_V7X):
    return pltpu.CompilerParams(dimension_semantics=("arbitrary",) * n_axes, vmem_limit_bytes=vmem)


def _layer_norm_rows(x, g, b):
    mu = jnp.mean(x, axis=-1, keepdims=True)
    xc = x - mu
    var = jnp.mean(xc * xc, axis=-1, keepdims=True)
    return xc * lax.rsqrt(var + LN_EPS) * g + b


def _sigmoid(x):
    return 1.0 / (1.0 + jnp.exp(-x))


def _ln_in_kernel(x_ref, g_ref, b_ref, hf_ref, hb_ref):
    y = _layer_norm_rows(x_ref[...], g_ref[...], b_ref[...])
    hf_ref[...] = y
    hb_ref[...] = y.astype(BF16)


def ln_in(x, g, b, tm=512):
    T, D = x.shape
    row = pl.BlockSpec((tm, D), lambda i: (i, 0))
    vec = pl.BlockSpec((1, D), lambda i: (0, 0))
    return pl.pallas_call(
        _ln_in_kernel, grid=(T // tm,), in_specs=[row, vec, vec], out_specs=[row, row],
        out_shape=[jax.ShapeDtypeStruct((T, D), F32), jax.ShapeDtypeStruct((T, D), BF16)],
        compiler_params=_cparams(), name="ln_in")(x, g.reshape(1, D), b.reshape(1, D))


def _mm_kernel(x_ref, w_ref, o_ref):
    o_ref[...] = jnp.dot(x_ref[...], w_ref[...], preferred_element_type=F32).astype(o_ref.dtype)


def matmul(x, w, col0, N, out_dtype, tm, tn, name):
    M, K = x.shape
    assert col0 % tn == 0 and N % tn == 0
    off = col0 // tn
    return pl.pallas_call(
        _mm_kernel, grid=(M // tm, N // tn),
        in_specs=[pl.BlockSpec((tm, K), lambda i, j: (i, 0)), pl.BlockSpec((K, tn), lambda i, j: (0, off + j))],
        out_specs=pl.BlockSpec((tm, tn), lambda i, j: (i, j)),
        out_shape=jax.ShapeDtypeStruct((M, N), out_dtype),
        compiler_params=_cparams(2), name=name)(x, w)


def _mm_rms_kernel(x_ref, w_ref, g_ref, o_ref):
    c = jnp.dot(x_ref[...], w_ref[...], preferred_element_type=F32)
    y = c * lax.rsqrt(jnp.mean(c * c, axis=-1, keepdims=True) + RMS_EPS) * g_ref[...]
    o_ref[...] = y.astype(o_ref.dtype)


def matmul_rms(x, w, col0, N, g, tm=1024):
    M, K = x.shape
    assert col0 % N == 0
    return pl.pallas_call(
        _mm_rms_kernel, grid=(M // tm,),
        in_specs=[pl.BlockSpec((tm, K), lambda i: (i, 0)), pl.BlockSpec((K, N), lambda i: (0, col0 // N)),
                  pl.BlockSpec((1, N), lambda i: (0, 0))],
        out_specs=pl.BlockSpec((tm, N), lambda i: (i, 0)),
        out_shape=jax.ShapeDtypeStruct((M, N), BF16),
        compiler_params=_cparams(), name="proj_kv_rms")(x, w, g.reshape(1, N))


def _attn_kernel(q_ref, qi_ref, wt_ref, kidx_ref, c_ref, ct_ref, wukt_ref, wuv_ref, o_ref,
                 sc_ref, big_ref, bias_ref, acc_ref, m_ref, l_ref, qlt_ref, qis_ref, thr_ref, j_ref,
                 *, QB, TK, NSEL):
    H = N_HEADS
    G = max(1, MXU_WIDTH_V7X // QB)
    NG = H // G
    b = pl.program_id(0)
    lmax = (b + 1) * QB
    nk = lax.div(lmax + (TK - 1), TK)
    ksel = float(NSEL)
    neg_inf = float("-inf")
    fmax = float(jnp.finfo(jnp.float32).max)

    tpos = b * QB + lax.broadcasted_iota(I32, (1, QB), 1)
    limit = (lax.shift_right_logical(tpos, 6) + 1) * CHUNK
    limit_f = limit.astype(F32)

    for h in range(H):
        qis_ref[h * QB:(h + 1) * QB, :] = qi_ref[:, h * IDX_DIM:(h + 1) * IDX_DIM]
        qlat_t = lax.dot_general(wukt_ref[h], q_ref[:, h * HEAD_DIM:(h + 1) * HEAD_DIM], NT_DIMS,
                                 preferred_element_type=F32)
        qlt_ref[h // G, :, (h % G) * QB:(h % G + 1) * QB] = (qlat_t * (ATTN_SCALE * LOG2_E)).astype(BF16)
    wscale = (IDX_HEADS ** -0.5) * (IDX_DIM ** -0.5)
    wrows = [wt_ref[h:h + 1, :] * wscale for h in range(H)]

    RC = 128

    def p1(kt, carry):
        k0 = pl.multiple_of(kt * TK, TK)
        big_ref[...] = lax.dot_general(kidx_ref[pl.ds(k0, TK), :], qis_ref[...], NT_DIMS,
                                       preferred_element_type=F32)
        for r in range(TK // RC):
            rows = slice(r * RC, (r + 1) * RC)
            acc = jnp.zeros((RC, QB), F32)
            for h in range(H):
                acc = acc + wrows[h] * jnp.maximum(big_ref[rows, h * QB:(h + 1) * QB], 0.0)
            key = k0 + r * RC + lax.broadcasted_iota(I32, (RC, QB), 0)
            sc_ref[kt, rows, :] = jnp.where(key < limit, acc, neg_inf)
        return carry

    lax.fori_loop(0, nk, p1, 0)

    CR = 32

    def count_ge(t):
        tb = jnp.broadcast_to(t, (CR, QB))

        def body(kt, cnt):
            hit = jnp.where(sc_ref[kt].reshape(TK // CR, CR, QB) >= tb, 1.0, 0.0)
            return cnt + jnp.sum(hit, axis=0)

        cnt = lax.fori_loop(0, nk, body, jnp.zeros((CR, QB), F32))
        return jnp.sum(cnt, axis=0, keepdims=True)

    def minmax_body(kt, carry):
        mn, mx = carry
        x = sc_ref[kt]
        mx = jnp.maximum(mx, jnp.max(x, axis=0, keepdims=True))
        mn = jnp.minimum(mn, jnp.min(jnp.where(x == neg_inf, fmax, x), axis=0, keepdims=True))
        return mn, mx

    mn, mx = lax.fori_loop(0, nk, minmax_body,
                           (jnp.full((1, QB), fmax, F32), jnp.full((1, QB), neg_inf, F32)))

    c_top = count_ge(mx)
    one = jnp.ones((1, QB), F32)
    zero = jnp.zeros((1, QB), F32)
    all_sel = jnp.where(limit_f <= ksel, one, zero)
    top_tie = (1.0 - all_sel) * jnp.where(c_top >= ksel, one, zero)
    lo0 = jnp.where(top_tie > 0, mx, mn)
    c_hi0 = jnp.where(top_tie > 0, zero, c_top)
    thr0 = jnp.where(all_sel > 0, -fmax, mx)
    tie0 = top_tie * jnp.where(c_top > ksel, one, zero)
    done0 = jnp.maximum(all_sel, top_tie)

    def bis_cond(st):
        return jnp.logical_and(st[0] > 0.5, st[1] < 400)

    def bis_body(st):
        _, it, lo, hi, c_hi, thr, tie, done = st
        n_act = jnp.max(1.0 - done)
        act = done < 0.5
        mid = 0.5 * lo + 0.5 * hi
        adjacent = jnp.logical_or(mid <= lo, mid >= hi)
        c = count_ge(mid)
        ge = c >= ksel
        exact = c == ksel
        stop = jnp.logical_or(exact, adjacent)
        fin_tie = jnp.logical_and(act, jnp.logical_and(adjacent, jnp.logical_not(exact)))
        thr = jnp.where(jnp.logical_and(act, exact), mid, jnp.where(fin_tie, lo, thr))
        tie = jnp.where(fin_tie, one, tie)
        upd = jnp.logical_and(act, jnp.logical_not(stop))
        up_lo = jnp.logical_and(upd, ge)
        up_hi = jnp.logical_and(upd, jnp.logical_not(ge))
        lo = jnp.where(up_lo, mid, lo)
        hi = jnp.where(up_hi, mid, hi)
        c_hi = jnp.where(up_hi, c, c_hi)
        done = jnp.where(jnp.logical_and(act, stop), one, done)
        return n_act, it + 1, lo, hi, c_hi, thr, tie, done

    st = lax.while_loop(bis_cond, bis_body,
                        (jnp.max(1.0 - done0), jnp.int32(0), lo0, mx, c_hi0, thr0, tie0, done0))
    _, _, _, _, c_hi, thr, tie, _ = st
    thr_ref[...] = thr
    big_j = float(2 ** 30)
    j_ref[...] = jnp.full((1, QB), big_j, F32)

    @pl.when(jnp.max(tie) > 0.5)
    def _():
        need = ksel - c_hi

        def count_tie_below(jb):
            def body(kt, cnt):
                keyf = (kt * TK + lax.broadcasted_iota(I32, (TK, QB), 0)).astype(F32)
                hit = jnp.logical_and(sc_ref[kt] == thr, keyf < jb)
                return cnt + jnp.sum(jnp.where(hit, 1.0, 0.0), axis=0, keepdims=True)

            return lax.fori_loop(0, nk, body, jnp.zeros((1, QB), F32))

        def jbody(_, carry):
            jlo, jhi = carry
            jm = jnp.floor((jlo + jhi) * 0.5)
            ok = count_tie_below(jm) >= need
            return jnp.where(ok, jlo, jm), jnp.where(ok, jm, jhi)

        n_steps = max(1, int(sc_ref.shape[0] * TK).bit_length())
        _, jhi = lax.fori_loop(0, n_steps, jbody, (zero, zero + (nk * TK).astype(F32)))
        j_ref[...] = jnp.where(tie > 0.5, jhi, big_j)

    m_ref[...] = jnp.full(m_ref.shape, -1e30, F32)
    l_ref[...] = jnp.zeros(l_ref.shape, F32)
    acc_ref[...] = jnp.zeros(acc_ref.shape, F32)

    def p3(kt, carry):
        k0 = pl.multiple_of(kt * TK, TK)
        ck = c_ref[pl.ds(k0, TK), :]
        ckt = ct_ref[kt]
        x = sc_ref[kt]
        keyf = (k0 + lax.broadcasted_iota(I32, (TK, QB), 0)).astype(F32)
        t = thr_ref[...]
        sel = jnp.logical_or(x > t, jnp.logical_and(x == t, keyf < j_ref[...]))
        bias_ref[...] = jnp.where(sel, 0.0, neg_inf)
        for pr in range(NG):
            s = jnp.dot(ck, qlt_ref[pr], preferred_element_type=F32)
            bias = bias_ref[...]
            if G == 1:
                s = s + bias
            else:
                s = jnp.concatenate([s[:, g * QB:(g + 1) * QB] + bias for g in range(G)], axis=1)
            m_old = m_ref[pr]
            m_new = jnp.maximum(m_old, jnp.max(s, axis=0, keepdims=True))
            p = jnp.exp2(s - m_new)
            alpha = jnp.exp2(m_old - m_new)
            l_ref[pr] = alpha * l_ref[pr] + jnp.sum(p, axis=0, keepdims=True)
            m_ref[pr] = m_new
            pv = jnp.dot(ckt, p.astype(BF16), preferred_element_type=F32)
            acc_ref[pr] = alpha * acc_ref[pr] + pv
        return carry

    lax.fori_loop(0, nk, p3, 0)

    for h in range(H):
        cols = slice((h % G) * QB, (h % G + 1) * QB)
        o_lat_t = acc_ref[h // G, :, cols] / l_ref[h // G, :, cols]
        o_ref[:, h * HEAD_DIM:(h + 1) * HEAD_DIM] = jnp.dot(
            o_lat_t.T.astype(BF16), wuv_ref[h], preferred_element_type=F32).astype(o_ref.dtype)


def dsa_attention(qq, kw, c, w_uk, w_uv, qb=ATTN_QB, tk=ATTN_TK):
    T = c.shape[0]
    tk = min(tk, T)
    nsel = min(TOPK_KEYS_MAX, T // 4)
    assert qb % CHUNK == 0 and T % tk == 0 and T % qb == 0 and tk % qb == 0
    H = N_HEADS
    hg = max(1, MXU_WIDTH_V7X // qb)
    kidx = kw[:, :IDX_DIM].astype(BF16)
    w_t = kw[:, IDX_DIM:IDX_DIM + IDX_HEADS].T
    c_t = c.reshape(T // tk, tk, KV_LATENT).transpose(0, 2, 1)
    kernel = functools.partial(_attn_kernel, QB=qb, TK=tk, NSEL=nsel)
    full = lambda shape: pl.BlockSpec(shape, lambda i: (0,) * len(shape))
    return pl.pallas_call(
        kernel, grid=(T // qb,),
        in_specs=[pl.BlockSpec((qb, Q_DIM), lambda i: (i, 0)),
                  pl.BlockSpec((qb, IDX_Q_DIM), lambda i: (i, Q_DIM // IDX_Q_DIM)),
                  pl.BlockSpec((H, qb), lambda i: (0, i)),
                  full((T, IDX_DIM)), full((T, KV_LATENT)), full((T // tk, KV_LATENT, tk)),
                  full((H, KV_LATENT, HEAD_DIM)), full((H, KV_LATENT, HEAD_DIM))],
        out_specs=pl.BlockSpec((qb, Q_DIM), lambda i: (i, 0)),
        out_shape=jax.ShapeDtypeStruct((T, Q_DIM), BF16),
        scratch_shapes=[pltpu.VMEM((T // tk, tk, qb), F32),
                        pltpu.VMEM((tk, H * qb), F32),
                        pltpu.VMEM((tk, qb), F32),
                        pltpu.VMEM((H // hg, KV_LATENT, hg * qb), F32),
                        pltpu.VMEM((H // hg, 1, hg * qb), F32), pltpu.VMEM((H // hg, 1, hg * qb), F32),
                        pltpu.VMEM((H // hg, KV_LATENT, hg * qb), BF16),
                        pltpu.VMEM((H * qb, IDX_DIM), BF16),
                        pltpu.VMEM((1, qb), F32), pltpu.VMEM((1, qb), F32)],
        compiler_params=_cparams(), name="dsa_attention")(
            qq, qq, w_t, kidx, c, c_t, w_uk.transpose(0, 2, 1).astype(BF16), w_uv.astype(BF16))


def _gelu_tanh(x):
    return 0.5 * x * (1.0 + jnp.tanh(0.7978845608028654 * (x + 0.044715 * (x * x * x))))


def _lru_kernel(xr_ref, yg_ref, cw_ref, cb_ref, wa_ref, ba_ref, wx_ref, bx_ref, lam_ref, o_ref,
                ext_ref, a_ref, u_ref, hs_ref, h_ref, *, TT):
    i = pl.program_id(0)
    C = LRU_WIDTH

    @pl.when(i == 0)
    def _():
        ext_ref[0:8, :] = jnp.zeros((8, C), F32)
        h_ref[...] = jnp.zeros((1, C), F32)

    ext_ref[8:8 + TT, :] = xr_ref[...]
    xc = jnp.zeros((TT, C), F32) + cb_ref[...]
    for j in range(CONV_WIDTH):
        off = 8 - (CONV_WIDTH - 1) + j
        xc = xc + cw_ref[j:j + 1, :] * ext_ref[off:off + TT, :]
    ext_ref[0:8, :] = ext_ref[TT:TT + 8, :]

    lam = lam_ref[...]
    softplus_neg = jnp.maximum(-lam, 0.0) + jnp.log1p(jnp.exp(-jnp.abs(lam)))
    xcb = xc.astype(BF16)
    for n in range(LRU_BLOCKS):
        cols = slice(n * LRU_BLOCK_DIM, (n + 1) * LRU_BLOCK_DIM)
        xb = xcb[:, cols]
        r = _sigmoid(jnp.dot(xb, wa_ref[n], preferred_element_type=F32) + ba_ref[:, cols])
        g = _sigmoid(jnp.dot(xb, wx_ref[n], preferred_element_type=F32) + bx_ref[:, cols])
        log_a = (-RG_C) * r * softplus_neg[:, cols]
        a_ref[:, cols] = jnp.exp(log_a)
        th = jnp.tanh(log_a)
        u_ref[:, cols] = jnp.sqrt((-2.0 * th) / (1.0 - th)) * (g * xc[:, cols])

    def step(t, h):
        h = a_ref[pl.ds(t, 1), :] * h + u_ref[pl.ds(t, 1), :]
        hs_ref[pl.ds(t, 1), :] = h
        return h

    h_ref[...] = lax.fori_loop(0, TT, step, h_ref[...], unroll=8)
    o_ref[...] = (hs_ref[...] * _gelu_tanh(yg_ref[...])).astype(o_ref.dtype)


def rg_lru(zz, conv_w, conv_b, w_a, b_a, w_x, b_x, lam, tt=256):
    T = zz.shape[0]
    tt = min(tt, T)
    C = LRU_WIDTH
    vec = pl.BlockSpec((1, C), lambda i: (0, 0))
    blk = pl.BlockSpec((LRU_BLOCKS, LRU_BLOCK_DIM, LRU_BLOCK_DIM), lambda i: (0, 0, 0))
    return pl.pallas_call(
        functools.partial(_lru_kernel, TT=tt), grid=(T // tt,),
        in_specs=[pl.BlockSpec((tt, C), lambda i: (i, 0)), pl.BlockSpec((tt, C), lambda i: (i, 1)),
                  pl.BlockSpec((CONV_WIDTH, C), lambda i: (0, 0)), vec, blk, vec, blk, vec, vec],
        out_specs=pl.BlockSpec((tt, C), lambda i: (i, 0)),
        out_shape=jax.ShapeDtypeStruct((T, C), BF16),
        scratch_shapes=[pltpu.VMEM((tt + 8, C), F32), pltpu.VMEM((tt, C), F32), pltpu.VMEM((tt, C), F32),
                        pltpu.VMEM((tt, C), F32), pltpu.VMEM((1, C), F32)],
        compiler_params=_cparams(), name="rg_lru")(
            zz, zz, conv_w, conv_b.reshape(1, C), w_a.astype(BF16), b_a.reshape(1, C),
            w_x.astype(BF16), b_x.reshape(1, C), lam.reshape(1, C))


def _merge_kernel(a_ref, l_ref, wa_ref, wb_ref, ga_ref, gb_ref, o_ref):
    ya = jnp.dot(a_ref[...], wa_ref[...], preferred_element_type=F32)
    yb = jnp.dot(l_ref[...], wb_ref[...], preferred_element_type=F32)
    o_ref[...] = (_sigmoid(ga_ref[...]) * ya + _sigmoid(gb_ref[...]) * yb).astype(o_ref.dtype)


def merge_branches(attn, lru, w_a, w_b, zz, tm=1024, tn=512):
    T, D = attn.shape
    nb = D // tn
    return pl.pallas_call(
        _merge_kernel, grid=(T // tm, nb),
        in_specs=[pl.BlockSpec((tm, D), lambda i, j: (i, 0)), pl.BlockSpec((tm, D), lambda i, j: (i, 0)),
                  pl.BlockSpec((D, tn), lambda i, j: (0, j)), pl.BlockSpec((D, tn), lambda i, j: (0, j)),
                  pl.BlockSpec((tm, tn), lambda i, j: (i, 2 * nb + j)),
                  pl.BlockSpec((tm, tn), lambda i, j: (i, 3 * nb + j))],
        out_specs=pl.BlockSpec((tm, tn), lambda i, j: (i, j)),
        out_shape=jax.ShapeDtypeStruct((T, D), BF16),
        compiler_params=_cparams(2), name="merge_branches")(attn, lru, w_a, w_b, zz, zz)


def _post1_kernel(m_ref, w_ref, h_ref, g_ref, b_ref, o_ref):
    y = ALPHA * h_ref[...] + jnp.dot(m_ref[...], w_ref[...], preferred_element_type=F32)
    o_ref[...] = _layer_norm_rows(y, g_ref[...], b_ref[...])


def out_proj_ln(merged, w_out, h0, g, b, tm=512):
    T, D = merged.shape
    row = lambda dt: pl.BlockSpec((tm, D), lambda i: (i, 0))
    vec = pl.BlockSpec((1, D), lambda i: (0, 0))
    return pl.pallas_call(
        _post1_kernel, grid=(T // tm,),
        in_specs=[row(BF16), pl.BlockSpec((D, D), lambda i: (0, 0)), row(F32), vec, vec],
        out_specs=row(F32), out_shape=jax.ShapeDtypeStruct((T, D), F32),
        compiler_params=_cparams(), name="out_proj_ln1")(merged, w_out, h0, g.reshape(1, D), b.reshape(1, D))


def _router_kernel(h_ref, wr_ref, rb_ref, eidx_ref, gate_ref, pos_ref, cnt_ref, carry_ref, *, TM):
    i = pl.program_id(0)
    E = N_EXPERTS
    per_group = E // N_GROUPS
    neg_inf = float("-inf")

    @pl.when(i == 0)
    def _():
        carry_ref[...] = jnp.zeros((1, E), F32)

    h = h_ref[...]
    hh = h.astype(BF16)
    hl = (h - hh.astype(F32)).astype(BF16)
    w = wr_ref[...]
    wh = w.astype(BF16)
    wl = (w - wh.astype(F32)).astype(BF16)
    logits = (jnp.dot(hh, wh, preferred_element_type=F32)
              + (jnp.dot(hh, wl, preferred_element_type=F32) + jnp.dot(hl, wh, preferred_element_type=F32)))
    scores = _sigmoid(logits)
    biased = scores + rb_ref[...]
    lane = lax.broadcasted_iota(I32, (TM, E), 1)

    def first_argmax(v):
        m = jnp.max(v, axis=1, keepdims=True)
        idx = jnp.min(jnp.where(v == m, lane, E), axis=1, keepdims=True)
        return m, idx

    gscore = []
    for g in range(N_GROUPS):
        in_g = jnp.logical_and(lane >= g * per_group, lane < (g + 1) * per_group)
        vg = jnp.where(in_g, biased, neg_inf)
        m1, i1 = first_argmax(vg)
        m2 = jnp.max(jnp.where(lane == i1, neg_inf, vg), axis=1, keepdims=True)
        gscore.append(m1 + m2)
    cand = jnp.full((TM, E), neg_inf, F32)
    for g in range(N_GROUPS):
        rank = jnp.zeros((TM, 1), F32)
        for g2 in range(N_GROUPS):
            if g2 == g:
                continue
            ahead = (gscore[g2] >= gscore[g]) if g2 < g else (gscore[g2] > gscore[g])
            rank = rank + jnp.where(ahead, 1.0, 0.0)
        in_g = jnp.logical_and(lane >= g * per_group, lane < (g + 1) * per_group)
        keep = jnp.logical_and(in_g, rank < float(TOPK_GROUPS))
        cand = jnp.where(keep, biased, cand)

    lane_k = lax.broadcasted_iota(I32, (TM, TOP_K), 1)
    sel = jnp.zeros((TM, E), F32)
    onehots, gates = [], []
    eidx = jnp.zeros((TM, TOP_K), I32)
    for k in range(TOP_K):
        _, ik = first_argmax(cand)
        oh = lane == ik
        onehots.append(oh)
        gates.append(jnp.sum(jnp.where(oh, scores, 0.0), axis=1, keepdims=True))
        cand = jnp.where(oh, neg_inf, cand)
        sel = sel + jnp.where(oh, 1.0, 0.0)
        eidx = jnp.where(lane_k == k, ik, eidx)
    gsum = gates[0]
    for k in range(1, TOP_K):
        gsum = gsum + gates[k]

    r_i = lax.broadcasted_iota(I32, (TM, TM), 0)
    c_i = lax.broadcasted_iota(I32, (TM, TM), 1)
    tri = jnp.where(c_i < r_i, 1.0, 0.0).astype(BF16)
    prefix = jnp.dot(tri, sel.astype(BF16), preferred_element_type=F32) + carry_ref[...]
    gate = jnp.zeros((TM, TOP_K), F32)
    pos = jnp.zeros((TM, TOP_K), F32)
    for k in range(TOP_K):
        gk = gates[k] / gsum * ROUTE_SCALE
        pk = jnp.sum(jnp.where(onehots[k], prefix, 0.0), axis=1, keepdims=True)
        gate = jnp.where(lane_k == k, gk, gate)
        pos = jnp.where(lane_k == k, pk, pos)
    eidx_ref[...] = eidx
    gate_ref[...] = gate
    pos_ref[...] = pos.astype(I32)
    carry_ref[...] = carry_ref[...] + jnp.sum(sel, axis=0, keepdims=True)
    cnt_ref[...] = carry_ref[...]


def router(h1, w_router, router_bias, tm=512):
    T, D = h1.shape
    tm = min(tm, T)
    E = N_EXPERTS
    out8 = pl.BlockSpec((tm, TOP_K), lambda i: (i, 0))
    return pl.pallas_call(
        functools.partial(_router_kernel, TM=tm), grid=(T // tm,),
        in_specs=[pl.BlockSpec((tm, D), lambda i: (i, 0)), pl.BlockSpec((D, E), lambda i: (0, 0)),
                  pl.BlockSpec((1, E), lambda i: (0, 0))],
        out_specs=[out8, out8, out8, pl.BlockSpec((1, E), lambda i: (0, 0))],
        out_shape=[jax.ShapeDtypeStruct((T, TOP_K), I32), jax.ShapeDtypeStruct((T, TOP_K), F32),
                   jax.ShapeDtypeStruct((T, TOP_K), I32), jax.ShapeDtypeStruct((1, E), F32)],
        scratch_shapes=[pltpu.VMEM((1, E), F32)],
        compiler_params=_cparams(), name="router")(h1, w_router, router_bias.reshape(1, E))


def _row_copy(src, src_row, dst, dst_row, sem):
    return pltpu.make_async_copy(src.at[pl.ds(src_row, 1)], dst.at[pl.ds(dst_row, 1)], sem)


def _dest_row(starts_ref, eidx_ref, pos_ref, i):
    return starts_ref[eidx_ref[i]] + pos_ref[i]


def _dispatch_kernel(starts_ref, eidx_ref, pos_ref, h_ref, xs_hbm, sem, *, TM):
    def start(n, carry):
        for k in range(TOP_K):
            d = _dest_row(starts_ref, eidx_ref, pos_ref, n * TOP_K + k)
            _row_copy(h_ref, n, xs_hbm, d, sem).start(priority=k % 2)
        return carry

    def wait(n, carry):
        for k in range(TOP_K):
            _row_copy(h_ref, 0, xs_hbm, 0, sem).wait()
        return carry

    lax.fori_loop(0, TM, start, 0)
    lax.fori_loop(0, TM, wait, 0)


def _route_specs(tm):
    idx = pl.BlockSpec((tm * TOP_K,), lambda i: (i,), memory_space=pltpu.SMEM)
    return [pl.BlockSpec((N_EXPERTS,), lambda i: (0,), memory_space=pltpu.SMEM), idx, idx]


def dispatch(starts, eidx, pos, h1, tm=ROW_TILE):
    T, D = h1.shape
    return pl.pallas_call(
        functools.partial(_dispatch_kernel, TM=tm), grid=(T // tm,),
        in_specs=_route_specs(tm) + [pl.BlockSpec((tm, D), lambda i: (i, 0))],
        out_specs=pl.BlockSpec(memory_space=pl.ANY),
        out_shape=jax.ShapeDtypeStruct((T * TOP_K, D), h1.dtype),
        scratch_shapes=[pltpu.SemaphoreType.DMA(())],
        compiler_params=_cparams(), name="moe_dispatch")(starts, eidx, pos, h1)


VISIT_FIRST_OF_TILE = 1
VISIT_FIRST_OF_EXPERT = 2
VISIT_ACTIVE = 4
VISIT_HAS_NEXT_EXPERT = 8


def _expert_kernel(vt_ref, ve_ref, vlo_ref, vhi_ref, vfl_ref, ven_ref, vsl_ref,
                   x_ref, wg_hbm, wu_hbm, wd_hbm, o_ref,
                   wgf_ref, wuf_ref, wdf_ref, wgb_ref, wub_ref, wdb_ref, wsem, *, TE):
    v = pl.program_id(0)
    flags = vfl_ref[v]
    slot = vsl_ref[v]

    def weight_copies(expert, s):
        return (pltpu.make_async_copy(wg_hbm.at[expert], wgf_ref.at[s], wsem.at[0, s]),
                pltpu.make_async_copy(wu_hbm.at[expert], wuf_ref.at[s], wsem.at[1, s]),
                pltpu.make_async_copy(wd_hbm.at[expert], wdf_ref.at[s], wsem.at[2, s]))

    @pl.when(v == 0)
    def _():
        for cp in weight_copies(ve_ref[0], 0):
            cp.start()

    @pl.when((flags & VISIT_FIRST_OF_EXPERT) != 0)
    def _():
        for cp in weight_copies(ve_ref[v], slot):
            cp.wait()

        @pl.when((flags & VISIT_HAS_NEXT_EXPERT) != 0)
        def _():
            for cp in weight_copies(ven_ref[v], 1 - slot):
                cp.start(priority=1)

        wgb_ref[...] = wgf_ref[slot].astype(BF16)
        wub_ref[...] = wuf_ref[slot].astype(BF16)
        wdb_ref[...] = wdf_ref[slot].astype(BF16)

    @pl.when((flags & VISIT_ACTIVE) != 0)
    def _():
        x = x_ref[...].astype(BF16)
        g = jnp.dot(x, wgb_ref[...], preferred_element_type=F32)
        u = jnp.dot(x, wub_ref[...], preferred_element_type=F32)
        rows = vt_ref[v] * TE + lax.broadcasted_iota(I32, (TE, 1), 0)
        mine = jnp.logical_and(rows >= vlo_ref[v], rows < vhi_ref[v])
        mid = jnp.where(mine, (g * _sigmoid(g)) * u, 0.0).astype(BF16)
        y = jnp.dot(mid, wdb_ref[...], preferred_element_type=F32)

        @pl.when((flags & VISIT_FIRST_OF_TILE) != 0)
        def _():
            o_ref[...] = y

        @pl.when((flags & VISIT_FIRST_OF_TILE) == 0)
        def _():
            o_ref[...] = o_ref[...] + y


def expert_ffn(tables, xs, w_gate_e, w_up_e, w_down_e, te=EXPERT_TILE):
    R, D = xs.shape
    F = EXPERT_FF
    nv = tables[0].shape[0]
    row_map = lambda v, vt, *_: (vt[v], 0)
    hbm = pl.BlockSpec(memory_space=pl.ANY)
    grid_spec = pltpu.PrefetchScalarGridSpec(
        num_scalar_prefetch=len(tables), grid=(nv,),
        in_specs=[pl.BlockSpec((te, D), row_map), hbm, hbm, hbm],
        out_specs=pl.BlockSpec((te, D), row_map),
        scratch_shapes=[pltpu.VMEM((2, D, F), F32), pltpu.VMEM((2, D, F), F32), pltpu.VMEM((2, F, D), F32),
                        pltpu.VMEM((D, F), BF16), pltpu.VMEM((D, F), BF16), pltpu.VMEM((F, D), BF16),
                        pltpu.SemaphoreType.DMA((3, 2))])
    return pl.pallas_call(
        functools.partial(_expert_kernel, TE=te), grid_spec=grid_spec,
        out_shape=jax.ShapeDtypeStruct((R, D), F32),
        compiler_params=_cparams(), name="expert_ffn")(*tables, xs, w_gate_e, w_up_e, w_down_e)


def visit_tables(counts, n_rows, te=EXPERT_TILE):
    E = counts.shape[0]
    n_tiles = n_rows // te
    nv = n_tiles + E
    ends = jnp.cumsum(counts)
    starts = ends - counts
    first_tile = starts // te
    last_tile = jnp.maximum(ends - 1, 0) // te
    nvis = jnp.where(counts > 0, last_tile - first_tile + 1, 0)
    vis_end = jnp.cumsum(nvis)
    vis_start = vis_end - nvis
    total = vis_end[-1]
    ids = jnp.arange(E, dtype=I32)[None, :]
    v = jnp.arange(nv, dtype=I32)
    vc = jnp.minimum(v, total - 1)
    e = jnp.sum((vis_end[None, :] <= vc[:, None]).astype(I32), axis=1)
    onehot = e[:, None] == ids
    pick = lambda table: jnp.sum(jnp.where(onehot, table[None, :], 0), axis=1)
    tile = (pick(first_tile - vis_start) + vc).astype(I32)
    lo = jnp.maximum(pick(starts), tile * te).astype(I32)
    hi = jnp.minimum(pick(ends), (tile + 1) * te).astype(I32)
    e_next = jnp.min(jnp.where(jnp.logical_and(ids > e[:, None], nvis[None, :] > 0), ids, E), axis=1)
    n_before = jnp.sum(jnp.where(jnp.logical_and(ids < e[:, None], nvis[None, :] > 0), 1, 0), axis=1)
    active = v < total
    first_of_tile = jnp.concatenate([jnp.ones((1,), bool), tile[1:] != tile[:-1]])
    first_of_expert = jnp.concatenate([jnp.ones((1,), bool), e[1:] != e[:-1]])
    flags = jnp.where(active,
                      VISIT_ACTIVE + jnp.where(first_of_tile, VISIT_FIRST_OF_TILE, 0)
                      + jnp.where(first_of_expert, VISIT_FIRST_OF_EXPERT, 0)
                      + jnp.where(e_next < E, VISIT_HAS_NEXT_EXPERT, 0), 0).astype(I32)
    slot = (n_before % 2).astype(I32)
    return tile, e.astype(I32), lo, hi, flags, jnp.minimum(e_next, E - 1).astype(I32), slot


def _combine_kernel(starts_ref, eidx_ref, pos_ref, gate_ref, h_ref, wgs_ref, wus_ref, wds_ref, g_ref, b_ref,
                    ys_hbm, o_ref, buf_ref, sem, *, TM):
    def start(n, carry):
        for k in range(TOP_K):
            d = _dest_row(starts_ref, eidx_ref, pos_ref, n * TOP_K + k)
            _row_copy(ys_hbm, d, buf_ref.at[k], n, sem).start(priority=k % 2)
        return carry

    def wait(n, carry):
        for k in range(TOP_K):
            _row_copy(ys_hbm, 0, buf_ref.at[k], 0, sem).wait()
        return carry

    lax.fori_loop(0, TM, start, 0)
    h = h_ref[...]
    hb = h.astype(BF16)
    g = jnp.dot(hb, wgs_ref[...], preferred_element_type=F32)
    u = jnp.dot(hb, wus_ref[...], preferred_element_type=F32)
    shared = jnp.dot(((g * _sigmoid(g)) * u).astype(BF16), wds_ref[...], preferred_element_type=F32)
    lax.fori_loop(0, TM, wait, 0)
    routed = jnp.zeros(h.shape, F32)
    for k in range(TOP_K):
        routed = routed + buf_ref[k] * gate_ref[:, k:k + 1]
    y = ALPHA * h + (routed + shared)
    o_ref[...] = _layer_norm_rows(y, g_ref[...], b_ref[...])


def combine(starts, eidx, pos, gate, h1, w_gate_s, w_up_s, w_down_s, g, b, ys, tm=ROW_TILE):
    T, D = h1.shape
    F = SHARED_FF
    row = pl.BlockSpec((tm, D), lambda i: (i, 0))
    vec = pl.BlockSpec((1, D), lambda i: (0, 0))
    return pl.pallas_call(
        functools.partial(_combine_kernel, TM=tm), grid=(T // tm,),
        in_specs=_route_specs(tm) + [
            pl.BlockSpec((tm, TOP_K), lambda i: (i, 0)), row,
            pl.BlockSpec((D, F), lambda i: (0, 0)), pl.BlockSpec((D, F), lambda i: (0, 0)),
            pl.BlockSpec((F, D), lambda i: (0, 0)), vec, vec,
            pl.BlockSpec(memory_space=pl.ANY)],
        out_specs=row, out_shape=jax.ShapeDtypeStruct((T, D), F32),
        scratch_shapes=[pltpu.VMEM((TOP_K, tm, D), F32), pltpu.SemaphoreType.DMA(())],
        compiler_params=_cparams(), name="moe_combine_ln2")(
            starts, eidx, pos, gate, h1, w_gate_s.astype(BF16), w_up_s.astype(BF16), w_down_s.astype(BF16),
            g.reshape(1, D), b.reshape(1, D), ys)


def _mixer(h0f, h0b, w_in, kv_norm_g, w_uk, w_uv, conv_w, conv_b, w_rg_a, b_rg_a, w_rg_x, b_rg_x, rg_lambda,
           w_branch_a, w_branch_b, w_out, ln1_g, ln1_b):
    o_c = Q_DIM
    o_qi = o_c + KV_LATENT
    o_ki = o_qi + IDX_Q_DIM
    o_xr = o_ki + IDX_DIM + IDX_HEADS
    n_gate = 4 * LRU_WIDTH
    w_kw = jnp.pad(w_in[:, o_ki:o_xr], ((0, 0), (0, LANES - IDX_DIM - IDX_HEADS)))
    w_all = jnp.concatenate([w_in[:, o_xr:], w_in[:, :o_c], w_in[:, o_qi:o_ki], w_in[:, o_c:o_qi], w_kw],
                            axis=1).astype(BF16)
    c_qq = n_gate
    c_c = c_qq + Q_DIM + IDX_Q_DIM
    c_kw = c_c + KV_LATENT
    zz = matmul(h0b, w_all, 0, n_gate, F32, 1024, 1024, "proj_lru_gates")
    qq = matmul(h0b, w_all, c_qq, Q_DIM + IDX_Q_DIM, BF16, 1024, 1024, "proj_q")
    c = matmul_rms(h0b, w_all, c_c, KV_LATENT, kv_norm_g)
    kw = matmul(h0b, w_all, c_kw, LANES, F32, 1024, LANES, "proj_idx_k")
    attn = dsa_attention(qq, kw, c, w_uk, w_uv)
    lru = rg_lru(zz, conv_w, conv_b, w_rg_a, b_rg_a, w_rg_x, b_rg_x, rg_lambda)
    merged = merge_branches(attn, lru, w_branch_a.astype(BF16), w_branch_b.astype(BF16), zz)
    return out_proj_ln(merged, w_out.astype(BF16), h0f, ln1_g, ln1_b)


def _moe(h1, w_router, router_bias, w_gate_e, w_up_e, w_down_e, w_gate_s, w_up_s, w_down_s, ln2_g, ln2_b):
    T = h1.shape[0]
    eidx, gate, pos, cnt = router(h1, w_router, router_bias)
    counts = cnt[0].astype(I32)
    starts = jnp.cumsum(counts) - counts
    eidx = eidx.reshape(T * TOP_K)
    pos = pos.reshape(T * TOP_K)
    xs = dispatch(starts, eidx, pos, h1)
    ys = expert_ffn(visit_tables(counts, T * TOP_K), xs, w_gate_e, w_up_e, w_down_e)
    return combine(starts, eidx, pos, gate, h1, w_gate_s, w_up_s, w_down_s, ln2_g, ln2_b, ys)


def kernel(x, ln_in_g, ln_in_b, w_in, kv_norm_g, w_uk, w_uv, conv_w, conv_b, w_rg_a, b_rg_a, w_rg_x, b_rg_x,
           rg_lambda, w_branch_a, w_branch_b, w_out, ln1_g, ln1_b, w_router, router_bias, w_gate_e, w_up_e,
           w_down_e, w_gate_s, w_up_s, w_down_s, ln2_g, ln2_b):
    B, T, D = x.shape
    assert B == 1 and D == D_MODEL and w_in.shape[0] == DEPTH
    hf, hb = ln_in(x.reshape(T, D), ln_in_g, ln_in_b)
    for l in range(DEPTH):
        h1 = _mixer(hf, hb, w_in[l], kv_norm_g[l], w_uk[l], w_uv[l], conv_w[l], conv_b[l], w_rg_a[l], b_rg_a[l],
                    w_rg_x[l], b_rg_x[l], rg_lambda[l], w_branch_a[l], w_branch_b[l], w_out[l], ln1_g[l], ln1_b[l])
        hf = _moe(h1, w_router[l], router_bias[l], w_gate_e[l], w_up_e[l], w_down_e[l], w_gate_s[l], w_up_s[l],
                  w_down_s[l], ln2_g[l], ln2_b[l])
        hb = hf.astype(BF16)
    return hf.reshape(B, T, D)
```

```python
import functools

import jax
import jax.numpy as jnp
from jax import lax
from jax.experimental import pallas as pl
from jax.experimental.pallas import tpu as pltpu

F32 = jnp.float32
BF16 = jnp.bfloat16
I32 = jnp.int32

D_MODEL = 2048
CHUNK = 64
N_HEADS = 16
HEAD_DIM = 128
KV_LATENT = 256
IDX_HEADS = 16
IDX_DIM = 64
TOPK_KEYS_MAX = 256
ATTN_SCALE = HEAD_DIM ** -0.5
LRU_WIDTH = 2048
LRU_BLOCKS = 16
LRU_BLOCK_DIM = LRU_WIDTH // LRU_BLOCKS
CONV_WIDTH = 4
RG_C = 8.0
N_EXPERTS = 64
TOP_K = 8
N_GROUPS = 8
TOPK_GROUPS = 4
EXPERT_FF = 512
SHARED_FF = 512
ROUTE_SCALE = 2.5
LN_EPS = 1e-5
RMS_EPS = 1e-6
Q_DIM = N_HEADS * HEAD_DIM
IDX_Q_DIM = IDX_HEADS * IDX_DIM
DEPTH = 1
ALPHA = (2.0 * DEPTH) ** 0.25
LOG2_E = 1.4426950408889634

VMEM_LIMIT_V7X = 56 * 1024 * 1024
LANES = 128
MXU_WIDTH_V7X = 256

ATTN_QB = 256
ATTN_TK = 512
EXPERT_TILE = 256
ROW_TILE = 128

NT_DIMS = (((1,), (1,)), ((), ()))


def _cparams(n_axes=1, vmem=VMEM_LIMIT_V7X):
    return pltpu.CompilerParams(dimension_semantics=("arbitrary",) * n_axes, vmem_limit_bytes=vmem)


def _layer_norm_rows(x, g, b):
    mu = jnp.mean(x, axis=-1, keepdims=True)
    xc = x - mu
    var = jnp.mean(xc * xc, axis=-1, keepdims=True)
    return xc * lax.rsqrt(var + LN_EPS) * g + b


def _sigmoid(x):
    return 1.0 / (1.0 + jnp.exp(-x))


def _ln_in_kernel(x_ref, g_ref, b_ref, hf_ref, hb_ref):
    y = _layer_norm_rows(x_ref[...], g_ref[...], b_ref[...])
    hf_ref[...] = y
    hb_ref[...] = y.astype(BF16)


def ln_in(x, g, b, tm=512):
    T, D = x.shape
    row = pl.BlockSpec((tm, D), lambda i: (i, 0))
    vec = pl.BlockSpec((1, D), lambda i: (0, 0))
    return pl.pallas_call(
        _ln_in_kernel, grid=(T // tm,), in_specs=[row, vec, vec], out_specs=[row, row],
        out_shape=[jax.ShapeDtypeStruct((T, D), F32), jax.ShapeDtypeStruct((T, D), BF16)],
        compiler_params=_cparams(), name="ln_in")(x, g.reshape(1, D), b.reshape(1, D))


def _mm_kernel(x_ref, w_ref, o_ref):
    o_ref[...] = jnp.dot(x_ref[...], w_ref[...], preferred_element_type=F32).astype(o_ref.dtype)


def matmul(x, w, col0, N, out_dtype, tm, tn, name):
    M, K = x.shape
    assert col0 % tn == 0 and N % tn == 0
    off = col0 // tn
    return pl.pallas_call(
        _mm_kernel, grid=(M // tm, N // tn),
        in_specs=[pl.BlockSpec((tm, K), lambda i, j: (i, 0)), pl.BlockSpec((K, tn), lambda i, j: (0, off + j))],
        out_specs=pl.BlockSpec((tm, tn), lambda i, j: (i, j)),
        out_shape=jax.ShapeDtypeStruct((M, N), out_dtype),
        compiler_params=_cparams(2), name=name)(x, w)


def _mm_rms_kernel(x_ref, w_ref, g_ref, o_ref):
    c = jnp.dot(x_ref[...], w_ref[...], preferred_element_type=F32)
    y = c * lax.rsqrt(jnp.mean(c * c, axis=-1, keepdims=True) + RMS_EPS) * g_ref[...]
    o_ref[...] = y.astype(o_ref.dtype)


def matmul_rms(x, w, col0, N, g, tm=1024):
    M, K = x.shape
    assert col0 % N == 0
    return pl.pallas_call(
        _mm_rms_kernel, grid=(M // tm,),
        in_specs=[pl.BlockSpec((tm, K), lambda i: (i, 0)), pl.BlockSpec((K, N), lambda i: (0, col0 // N)),
                  pl.BlockSpec((1, N), lambda i: (0, 0))],
        out_specs=pl.BlockSpec((tm, N), lambda i: (i, 0)),
        out_shape=jax.ShapeDtypeStruct((M, N), BF16),
        compiler_params=_cparams(), name="proj_kv_rms")(x, w, g.reshape(1, N))


def _attn_kernel(q_ref, qi_ref, wt_ref, kidx_ref, c_ref, ct_ref, wukt_ref, wuv_ref, o_ref,
                 sc_ref, big_ref, bias_ref, acc_ref, m_ref, l_ref, qlt_ref, qis_ref, thr_ref, j_ref,
                 *, QB, TK, NSEL):
    H = N_HEADS
    G = max(1, MXU_WIDTH_V7X // QB)
    NG = H // G
    b = pl.program_id(0)
    lmax = (b + 1) * QB
    nk = lax.div(lmax + (TK - 1), TK)
    ksel = float(NSEL)
    neg_inf = float("-inf")
    fmax = float(jnp.finfo(jnp.float32).max)

    tpos = b * QB + lax.broadcasted_iota(I32, (1, QB), 1)
    limit = (lax.shift_right_logical(tpos, 6) + 1) * CHUNK
    limit_f = limit.astype(F32)

    for h in range(H):
        qis_ref[h * QB:(h + 1) * QB, :] = qi_ref[:, h * IDX_DIM:(h + 1) * IDX_DIM]
        qlat_t = lax.dot_general(wukt_ref[h], q_ref[:, h * HEAD_DIM:(h + 1) * HEAD_DIM], NT_DIMS,
                                 preferred_element_type=F32)
        qlt_ref[h // G, :, (h % G) * QB:(h % G + 1) * QB] = (qlat_t * (ATTN_SCALE * LOG2_E)).astype(BF16)
    wscale = (IDX_HEADS ** -0.5) * (IDX_DIM ** -0.5)
    wrows = [wt_ref[h:h + 1, :] * wscale for h in range(H)]

    RC = 128

    def p1(kt, carry):
        k0 = pl.multiple_of(kt * TK, TK)
        big_ref[...] = lax.dot_general(kidx_ref[pl.ds(k0, TK), :], qis_ref[...], NT_DIMS,
                                       preferred_element_type=F32)
        for r in range(TK // RC):
            rows = slice(r * RC, (r + 1) * RC)
            acc = jnp.zeros((RC, QB), F32)
            for h in range(H):
                acc = acc + wrows[h] * jnp.maximum(big_ref[rows, h * QB:(h + 1) * QB], 0.0)
            key = k0 + r * RC + lax.broadcasted_iota(I32, (RC, QB), 0)
            sc_ref[kt, rows, :] = jnp.where(key < limit, acc, neg_inf)
        return carry

    lax.fori_loop(0, nk, p1, 0)

    CR = 32

    def count_ge(t):
        tb = jnp.broadcast_to(t, (CR, QB))

        def body(kt, cnt):
            hit = jnp.where(sc_ref[kt].reshape(TK // CR, CR, QB) >= tb, 1.0, 0.0)
            return cnt + jnp.sum(hit, axis=0)

        cnt = lax.fori_loop(0, nk, body, jnp.zeros((CR, QB), F32))
        return jnp.sum(cnt, axis=0, keepdims=True)

    def minmax_body(kt, carry):
        mn, mx = carry
        x = sc_ref[kt]
        mx = jnp.maximum(mx, jnp.max(x, axis=0, keepdims=True))
        mn = jnp.minimum(mn, jnp.min(jnp.where(x == neg_inf, fmax, x), axis=0, keepdims=True))
        return mn, mx

    mn, mx = lax.fori_loop(0, nk, minmax_body,
                           (jnp.full((1, QB), fmax, F32), jnp.full((1, QB), neg_inf, F32)))

    c_top = count_ge(mx)
    one = jnp.ones((1, QB), F32)
    zero = jnp.zeros((1, QB), F32)
    all_sel = jnp.where(limit_f <= ksel, one, zero)
    top_tie = (1.0 - all_sel) * jnp.where(c_top >= ksel, one, zero)
    lo0 = jnp.where(top_tie > 0, mx, mn)
    c_hi0 = jnp.where(top_tie > 0, zero, c_top)
    thr0 = jnp.where(all_sel > 0, -fmax, mx)
    tie0 = top_tie * jnp.where(c_top > ksel, one, zero)
    done0 = jnp.maximum(all_sel, top_tie)

    def bis_cond(st):
        return jnp.logical_and(st[0] > 0.5, st[1] < 400)

    def bis_body(st):
        _, it, lo, hi, c_hi, thr, tie, done = st
        n_act = jnp.max(1.0 - done)
        act = done < 0.5
        mid = 0.5 * lo + 0.5 * hi
        adjacent = jnp.logical_or(mid <= lo, mid >= hi)
        c = count_ge(mid)
        ge = c >= ksel
        exact = c == ksel
        stop = jnp.logical_or(exact, adjacent)
        fin_tie = jnp.logical_and(act, jnp.logical_and(adjacent, jnp.logical_not(exact)))
        thr = jnp.where(jnp.logical_and(act, exact), mid, jnp.where(fin_tie, lo, thr))
        tie = jnp.where(fin_tie, one, tie)
        upd = jnp.logical_and(act, jnp.logical_not(stop))
        up_lo = jnp.logical_and(upd, ge)
        up_hi = jnp.logical_and(upd, jnp.logical_not(ge))
        lo = jnp.where(up_lo, mid, lo)
        hi = jnp.where(up_hi, mid, hi)
        c_hi = jnp.where(up_hi, c, c_hi)
        done = jnp.where(jnp.logical_and(act, stop), one, done)
        return n_act, it + 1, lo, hi, c_hi, thr, tie, done

    st = lax.while_loop(bis_cond, bis_body,
                        (jnp.max(1.0 - done0), jnp.int32(0), lo0, mx, c_hi0, thr0, tie0, done0))
    _, _, _, _, c_hi, thr, tie, _ = st
    thr_ref[...] = thr
    big_j = float(2 ** 30)
    j_ref[...] = jnp.full((1, QB), big_j, F32)

    @pl.when(jnp.max(tie) > 0.5)
    def _():
        need = ksel - c_hi

        def count_tie_below(jb):
            def body(kt, cnt):
                keyf = (kt * TK + lax.broadcasted_iota(I32, (TK, QB), 0)).astype(F32)
                hit = jnp.logical_and(sc_ref[kt] == thr, keyf < jb)
                return cnt + jnp.sum(jnp.where(hit, 1.0, 0.0), axis=0, keepdims=True)

            return lax.fori_loop(0, nk, body, jnp.zeros((1, QB), F32))

        def jbody(_, carry):
            jlo, jhi = carry
            jm = jnp.floor((jlo + jhi) * 0.5)
            ok = count_tie_below(jm) >= need
            return jnp.where(ok, jlo, jm), jnp.where(ok, jm, jhi)

        n_steps = max(1, int(sc_ref.shape[0] * TK).bit_length())
        _, jhi = lax.fori_loop(0, n_steps, jbody, (zero, zero + (nk * TK).astype(F32)))
        j_ref[...] = jnp.where(tie > 0.5, jhi, big_j)

    m_ref[...] = jnp.full(m_ref.shape, -1e30, F32)
    l_ref[...] = jnp.zeros(l_ref.shape, F32)
    acc_ref[...] = jnp.zeros(acc_ref.shape, F32)

    def p3(kt, carry):
        k0 = pl.multiple_of(kt * TK, TK)
        ck = c_ref[pl.ds(k0, TK), :]
        ckt = ct_ref[kt]
        x = sc_ref[kt]
        keyf = (k0 + lax.broadcasted_iota(I32, (TK, QB), 0)).astype(F32)
        t = thr_ref[...]
        sel = jnp.logical_or(x > t, jnp.logical_and(x == t, keyf < j_ref[...]))
        bias_ref[...] = jnp.where(sel, 0.0, neg_inf)
        for pr in range(NG):
            s = jnp.dot(ck, qlt_ref[pr], preferred_element_type=F32)
            bias = bias_ref[...]
            if G == 1:
                s = s + bias
            else:
                s = jnp.concatenate([s[:, g * QB:(g + 1) * QB] + bias for g in range(G)], axis=1)
            m_old = m_ref[pr]
            m_new = jnp.maximum(m_old, jnp.max(s, axis=0, keepdims=True))
            p = jnp.exp2(s - m_new)
            alpha = jnp.exp2(m_old - m_new)
            l_ref[pr] = alpha * l_ref[pr] + jnp.sum(p, axis=0, keepdims=True)
            m_ref[pr] = m_new
            pv = jnp.dot(ckt, p.astype(BF16), preferred_element_type=F32)
            acc_ref[pr] = alpha * acc_ref[pr] + pv
        return carry

    lax.fori_loop(0, nk, p3, 0)

    for h in range(H):
        cols = slice((h % G) * QB, (h % G + 1) * QB)
        o_lat_t = acc_ref[h // G, :, cols] / l_ref[h // G, :, cols]
        o_ref[:, h * HEAD_DIM:(h + 1) * HEAD_DIM] = jnp.dot(
            o_lat_t.T.astype(BF16), wuv_ref[h], preferred_element_type=F32).astype(o_ref.dtype)


def dsa_attention(qq, kw, c, w_uk, w_uv, qb=ATTN_QB, tk=ATTN_TK):
    T = c.shape[0]
    tk = min(tk, T)
    nsel = min(TOPK_KEYS_MAX, T // 4)
    assert qb % CHUNK == 0 and T % tk == 0 and T % qb == 0 and tk % qb == 0
    H = N_HEADS
    hg = max(1, MXU_WIDTH_V7X // qb)
    kidx = kw[:, :IDX_DIM].astype(BF16)
    w_t = kw[:, IDX_DIM:IDX_DIM + IDX_HEADS].T
    c_t = c.reshape(T // tk, tk, KV_LATENT).transpose(0, 2, 1)
    kernel = functools.partial(_attn_kernel, QB=qb, TK=tk, NSEL=nsel)
    full = lambda shape: pl.BlockSpec(shape, lambda i: (0,) * len(shape))
    return pl.pallas_call(
        kernel, grid=(T // qb,),
        in_specs=[pl.BlockSpec((qb, Q_DIM), lambda i: (i, 0)),
                  pl.BlockSpec((qb, IDX_Q_DIM), lambda i: (i, Q_DIM // IDX_Q_DIM)),
                  pl.BlockSpec((H, qb), lambda i: (0, i)),
                  full((T, IDX_DIM)), full((T, KV_LATENT)), full((T // tk, KV_LATENT, tk)),
                  full((H, KV_LATENT, HEAD_DIM)), full((H, KV_LATENT, HEAD_DIM))],
        out_specs=pl.BlockSpec((qb, Q_DIM), lambda i: (i, 0)),
        out_shape=jax.ShapeDtypeStruct((T, Q_DIM), BF16),
        scratch_shapes=[pltpu.VMEM((T // tk, tk, qb), F32),
                        pltpu.VMEM((tk, H * qb), F32),
                        pltpu.VMEM((tk, qb), F32),
                        pltpu.VMEM((H // hg, KV_LATENT, hg * qb), F32),
                        pltpu.VMEM((H // hg, 1, hg * qb), F32), pltpu.VMEM((H // hg, 1, hg * qb), F32),
                        pltpu.VMEM((H // hg, KV_LATENT, hg * qb), BF16),
                        pltpu.VMEM((H * qb, IDX_DIM), BF16),
                        pltpu.VMEM((1, qb), F32), pltpu.VMEM((1, qb), F32)],
        compiler_params=_cparams(), name="dsa_attention")(
            qq, qq, w_t, kidx, c, c_t, w_uk.transpose(0, 2, 1).astype(BF16), w_uv.astype(BF16))


def _gelu_tanh(x):
    return 0.5 * x * (1.0 + jnp.tanh(0.7978845608028654 * (x + 0.044715 * (x * x * x))))


def _lru_kernel(xr_ref, yg_ref, cw_ref, cb_ref, wa_ref, ba_ref, wx_ref, bx_ref, lam_ref, o_ref,
                ext_ref, a_ref, u_ref, hs_ref, h_ref, *, TT):
    i = pl.program_id(0)
    C = LRU_WIDTH

    @pl.when(i == 0)
    def _():
        ext_ref[0:8, :] = jnp.zeros((8, C), F32)
        h_ref[...] = jnp.zeros((1, C), F32)

    ext_ref[8:8 + TT, :] = xr_ref[...]
    xc = jnp.zeros((TT, C), F32) + cb_ref[...]
    for j in range(CONV_WIDTH):
        off = 8 - (CONV_WIDTH - 1) + j
        xc = xc + cw_ref[j:j + 1, :] * ext_ref[off:off + TT, :]
    ext_ref[0:8, :] = ext_ref[TT:TT + 8, :]

    lam = lam_ref[...]
    softplus_neg = jnp.maximum(-lam, 0.0) + jnp.log1p(jnp.exp(-jnp.abs(lam)))
    xcb = xc.astype(BF16)
    for n in range(LRU_BLOCKS):
        cols = slice(n * LRU_BLOCK_DIM, (n + 1) * LRU_BLOCK_DIM)
        xb = xcb[:, cols]
        r = _sigmoid(jnp.dot(xb, wa_ref[n], preferred_element_type=F32) + ba_ref[:, cols])
        g = _sigmoid(jnp.dot(xb, wx_ref[n], preferred_element_type=F32) + bx_ref[:, cols])
        log_a = (-RG_C) * r * softplus_neg[:, cols]
        a_ref[:, cols] = jnp.exp(log_a)
        th = jnp.tanh(log_a)
        u_ref[:, cols] = jnp.sqrt((-2.0 * th) / (1.0 - th)) * (g * xc[:, cols])

    def step(t, h):
        h = a_ref[pl.ds(t, 1), :] * h + u_ref[pl.ds(t, 1), :]
        hs_ref[pl.ds(t, 1), :] = h
        return h

    h_ref[...] = lax.fori_loop(0, TT, step, h_ref[...], unroll=8)
    o_ref[...] = (hs_ref[...] * _gelu_tanh(yg_ref[...])).astype(o_ref.dtype)


def rg_lru(zz, conv_w, conv_b, w_a, b_a, w_x, b_x, lam, tt=256):
    T = zz.shape[0]
    tt = min(tt, T)
    C = LRU_WIDTH
    vec = pl.BlockSpec((1, C), lambda i: (0, 0))
    blk = pl.BlockSpec((LRU_BLOCKS, LRU_BLOCK_DIM, LRU_BLOCK_DIM), lambda i: (0, 0, 0))
    return pl.pallas_call(
        functools.partial(_lru_kernel, TT=tt), grid=(T // tt,),
        in_specs=[pl.BlockSpec((tt, C), lambda i: (i, 0)), pl.BlockSpec((tt, C), lambda i: (i, 1)),
                  pl.BlockSpec((CONV_WIDTH, C), lambda i: (0, 0)), vec, blk, vec, blk, vec, vec],
        out_specs=pl.BlockSpec((tt, C), lambda i: (i, 0)),
        out_shape=jax.ShapeDtypeStruct((T, C), BF16),
        scratch_shapes=[pltpu.VMEM((tt + 8, C), F32), pltpu.VMEM((tt, C), F32), pltpu.VMEM((tt, C), F32),
                        pltpu.VMEM((tt, C), F32), pltpu.VMEM((1, C), F32)],
        compiler_params=_cparams(), name="rg_lru")(
            zz, zz, conv_w, conv_b.reshape(1, C), w_a.astype(BF16), b_a.reshape(1, C),
            w_x.astype(BF16), b_x.reshape(1, C), lam.reshape(1, C))


def _merge_kernel(a_ref, l_ref, wa_ref, wb_ref, ga_ref, gb_ref, o_ref):
    ya = jnp.dot(a_ref[...], wa_ref[...], preferred_element_type=F32)
    yb = jnp.dot(l_ref[...], wb_ref[...], preferred_element_type=F32)
    o_ref[...] = (_sigmoid(ga_ref[...]) * ya + _sigmoid(gb_ref[...]) * yb).astype(o_ref.dtype)


def merge_branches(attn, lru, w_a, w_b, zz, tm=1024, tn=512):
    T, D = attn.shape
    nb = D // tn
    return pl.pallas_call(
        _merge_kernel, grid=(T // tm, nb),
        in_specs=[pl.BlockSpec((tm, D), lambda i, j: (i, 0)), pl.BlockSpec((tm, D), lambda i, j: (i, 0)),
                  pl.BlockSpec((D, tn), lambda i, j: (0, j)), pl.BlockSpec((D, tn), lambda i, j: (0, j)),
                  pl.BlockSpec((tm, tn), lambda i, j: (i, 2 * nb + j)),
                  pl.BlockSpec((tm, tn), lambda i, j: (i, 3 * nb + j))],
        out_specs=pl.BlockSpec((tm, tn), lambda i, j: (i, j)),
        out_shape=jax.ShapeDtypeStruct((T, D), BF16),
        compiler_params=_cparams(2), name="merge_branches")(attn, lru, w_a, w_b, zz, zz)


def _post1_kernel(m_ref, w_ref, h_ref, g_ref, b_ref, o_ref):
    y = ALPHA * h_ref[...] + jnp.dot(m_ref[...], w_ref[...], preferred_element_type=F32)
    o_ref[...] = _layer_norm_rows(y, g_ref[...], b_ref[...])


def out_proj_ln(merged, w_out, h0, g, b, tm=512):
    T, D = merged.shape
    row = lambda dt: pl.BlockSpec((tm, D), lambda i: (i, 0))
    vec = pl.BlockSpec((1, D), lambda i: (0, 0))
    return pl.pallas_call(
        _post1_kernel, grid=(T // tm,),
        in_specs=[row(BF16), pl.BlockSpec((D, D), lambda i: (0, 0)), row(F32), vec, vec],
        out_specs=row(F32), out_shape=jax.ShapeDtypeStruct((T, D), F32),
        compiler_params=_cparams(), name="out_proj_ln1")(merged, w_out, h0, g.reshape(1, D), b.reshape(1, D))


def _router_kernel(h_ref, wrt_ref, rb_ref, eidx_ref, gate_ref, pos_ref, cnt_ref, carry_ref, *, TM):
    i = pl.program_id(0)
    E = N_EXPERTS
    per_group = E // N_GROUPS
    neg_inf = float("-inf")

    @pl.when(i == 0)
    def _():
        carry_ref[...] = jnp.zeros((E, 1), F32)

    h = h_ref[...]
    hh = h.astype(BF16)
    hl = (h - hh.astype(F32)).astype(BF16)
    w = wrt_ref[...]
    wh = w.astype(BF16)
    wl = (w - wh.astype(F32)).astype(BF16)
    nt = lambda a, b: lax.dot_general(a, b, NT_DIMS, preferred_element_type=F32)
    logits = nt(wh, hh) + (nt(wl, hh) + nt(wh, hl))
    scores = _sigmoid(logits)
    biased = scores + rb_ref[...]

    b3 = biased.reshape(N_GROUPS, per_group, TM)
    sub = lax.broadcasted_iota(I32, (N_GROUPS, per_group, TM), 1)
    m1 = jnp.max(b3, axis=1, keepdims=True)
    i1 = jnp.min(jnp.where(b3 == m1, sub, per_group), axis=1, keepdims=True)
    m2 = jnp.max(jnp.where(sub == i1, neg_inf, b3), axis=1, keepdims=True)
    gscore = (m1 + m2).reshape(N_GROUPS, TM)
    gid = lax.broadcasted_iota(I32, (N_GROUPS, TM), 0)
    rank = jnp.zeros((N_GROUPS, TM), F32)
    for g2 in range(N_GROUPS):
        other = gscore[g2:g2 + 1, :]
        ahead = jnp.logical_or(other > gscore, jnp.logical_and(other == gscore, gid > g2))
        rank = rank + jnp.where(ahead, 1.0, 0.0)
    keep = jnp.broadcast_to((rank < float(TOPK_GROUPS)).reshape(N_GROUPS, 1, TM), (N_GROUPS, per_group, TM))
    cand = jnp.where(keep, b3, neg_inf).reshape(E, TM)

    eid = lax.broadcasted_iota(I32, (E, TM), 0)
    sel = jnp.zeros((E, TM), F32)
    onehots, gates, picks = [], [], []
    for k in range(TOP_K):
        m = jnp.max(cand, axis=0, keepdims=True)
        ik = jnp.min(jnp.where(cand == m, eid, E), axis=0, keepdims=True)
        oh = eid == ik
        onehots.append(oh)
        picks.append(ik)
        gates.append(jnp.sum(jnp.where(oh, scores, 0.0), axis=0, keepdims=True))
        cand = jnp.where(oh, neg_inf, cand)
        sel = sel + jnp.where(oh, 1.0, 0.0)
    gsum = gates[0]
    for k in range(1, TOP_K):
        gsum = gsum + gates[k]

    r_i = lax.broadcasted_iota(I32, (TM, TM), 0)
    c_i = lax.broadcasted_iota(I32, (TM, TM), 1)
    before = jnp.where(r_i < c_i, 1.0, 0.0).astype(BF16)
    prefix = jnp.dot(sel.astype(BF16), before, preferred_element_type=F32) + carry_ref[...]
    eidx_ref[...] = jnp.concatenate(picks, axis=0)
    gate_ref[...] = jnp.concatenate([gates[k] / gsum * ROUTE_SCALE for k in range(TOP_K)], axis=0)
    pos_ref[...] = jnp.concatenate(
        [jnp.sum(jnp.where(onehots[k], prefix, 0.0), axis=0, keepdims=True) for k in range(TOP_K)],
        axis=0).astype(I32)
    carry_ref[...] = carry_ref[...] + jnp.sum(sel, axis=1, keepdims=True)
    cnt_ref[...] = carry_ref[...]


def router(h1, w_router, router_bias, tm=512):
    T, D = h1.shape
    tm = min(tm, T)
    E = N_EXPERTS
    out8 = pl.BlockSpec((TOP_K, tm), lambda i: (0, i))
    return pl.pallas_call(
        functools.partial(_router_kernel, TM=tm), grid=(T // tm,),
        in_specs=[pl.BlockSpec((tm, D), lambda i: (i, 0)), pl.BlockSpec((E, D), lambda i: (0, 0)),
                  pl.BlockSpec((E, 1), lambda i: (0, 0))],
        out_specs=[out8, out8, out8, pl.BlockSpec((E, 1), lambda i: (0, 0))],
        out_shape=[jax.ShapeDtypeStruct((TOP_K, T), I32), jax.ShapeDtypeStruct((TOP_K, T), F32),
                   jax.ShapeDtypeStruct((TOP_K, T), I32), jax.ShapeDtypeStruct((E, 1), F32)],
        scratch_shapes=[pltpu.VMEM((E, 1), F32)],
        compiler_params=_cparams(), name="router")(h1, w_router.T, router_bias.reshape(E, 1))


def _row_copy(src, src_row, dst, dst_row, sem):
    return pltpu.make_async_copy(src.at[pl.ds(src_row, 1)], dst.at[pl.ds(dst_row, 1)], sem)


def _dispatch_kernel(dest_ref, h_ref, xs_hbm, sem, *, TM):
    def start(n, carry):
        for k in range(TOP_K):
            _row_copy(h_ref, n, xs_hbm, dest_ref[k, n], sem).start(priority=k % 2)
        return carry

    def wait(n, carry):
        for k in range(TOP_K):
            _row_copy(h_ref, 0, xs_hbm, 0, sem).wait()
        return carry

    lax.fori_loop(0, TM, start, 0)
    lax.fori_loop(0, TM, wait, 0)


def _dest_spec(tm):
    return pl.BlockSpec((TOP_K, tm), lambda i: (0, i), memory_space=pltpu.SMEM)


def dispatch(dest, h1, tm=ROW_TILE):
    T, D = h1.shape
    return pl.pallas_call(
        functools.partial(_dispatch_kernel, TM=tm), grid=(T // tm,),
        in_specs=[_dest_spec(tm), pl.BlockSpec((tm, D), lambda i: (i, 0))],
        out_specs=pl.BlockSpec(memory_space=pl.ANY),
        out_shape=jax.ShapeDtypeStruct((T * TOP_K, D), h1.dtype),
        scratch_shapes=[pltpu.SemaphoreType.DMA(())],
        compiler_params=_cparams(), name="moe_dispatch")(dest, h1)


VISIT_FIRST_OF_TILE = 1
VISIT_FIRST_OF_EXPERT = 2
VISIT_ACTIVE = 4
VISIT_HAS_NEXT_EXPERT = 8


def _expert_kernel(vt_ref, ve_ref, vlo_ref, vhi_ref, vfl_ref, ven_ref, vsl_ref,
                   x_ref, wg_hbm, wu_hbm, wd_hbm, o_ref,
                   wgf_ref, wuf_ref, wdf_ref, wgb_ref, wub_ref, wdb_ref, wsem, *, TE):
    v = pl.program_id(0)
    flags = vfl_ref[v]
    slot = vsl_ref[v]

    def weight_copies(expert, s):
        return (pltpu.make_async_copy(wg_hbm.at[expert], wgf_ref.at[s], wsem.at[0, s]),
                pltpu.make_async_copy(wu_hbm.at[expert], wuf_ref.at[s], wsem.at[1, s]),
                pltpu.make_async_copy(wd_hbm.at[expert], wdf_ref.at[s], wsem.at[2, s]))

    @pl.when(v == 0)
    def _():
        for cp in weight_copies(ve_ref[0], 0):
            cp.start()

    @pl.when((flags & VISIT_FIRST_OF_EXPERT) != 0)
    def _():
        for cp in weight_copies(ve_ref[v], slot):
            cp.wait()

        @pl.when((flags & VISIT_HAS_NEXT_EXPERT) != 0)
        def _():
            for cp in weight_copies(ven_ref[v], 1 - slot):
                cp.start(priority=1)

        wgb_ref[...] = wgf_ref[slot].astype(BF16)
        wub_ref[...] = wuf_ref[slot].astype(BF16)
        wdb_ref[...] = wdf_ref[slot].astype(BF16)

    @pl.when((flags & VISIT_ACTIVE) != 0)
    def _():
        x = x_ref[...].astype(BF16)
        g = jnp.dot(x, wgb_ref[...], preferred_element_type=F32)
        u = jnp.dot(x, wub_ref[...], preferred_element_type=F32)
        rows = vt_ref[v] * TE + lax.broadcasted_iota(I32, (TE, 1), 0)
        mine = jnp.logical_and(rows >= vlo_ref[v], rows < vhi_ref[v])
        mid = jnp.where(mine, (g * _sigmoid(g)) * u, 0.0).astype(BF16)
        y = jnp.dot(mid, wdb_ref[...], preferred_element_type=F32)

        @pl.when((flags & VISIT_FIRST_OF_TILE) != 0)
        def _():
            o_ref[...] = y

        @pl.when((flags & VISIT_FIRST_OF_TILE) == 0)
        def _():
            o_ref[...] = o_ref[...] + y


def expert_ffn(tables, xs, w_gate_e, w_up_e, w_down_e, te=EXPERT_TILE):
    R, D = xs.shape
    F = EXPERT_FF
    nv = tables[0].shape[0]
    row_map = lambda v, vt, *_: (vt[v], 0)
    hbm = pl.BlockSpec(memory_space=pl.ANY)
    grid_spec = pltpu.PrefetchScalarGridSpec(
        num_scalar_prefetch=len(tables), grid=(nv,),
        in_specs=[pl.BlockSpec((te, D), row_map), hbm, hbm, hbm],
        out_specs=pl.BlockSpec((te, D), row_map),
        scratch_shapes=[pltpu.VMEM((2, D, F), F32), pltpu.VMEM((2, D, F), F32), pltpu.VMEM((2, F, D), F32),
                        pltpu.VMEM((D, F), BF16), pltpu.VMEM((D, F), BF16), pltpu.VMEM((F, D), BF16),
                        pltpu.SemaphoreType.DMA((3, 2))])
    return pl.pallas_call(
        functools.partial(_expert_kernel, TE=te), grid_spec=grid_spec,
        out_shape=jax.ShapeDtypeStruct((R, D), F32),
        compiler_params=_cparams(), name="expert_ffn")(*tables, xs, w_gate_e, w_up_e, w_down_e)


def visit_tables(counts, n_rows, te=EXPERT_TILE):
    E = counts.shape[0]
    n_tiles = n_rows // te
    nv = n_tiles + E
    ends = jnp.cumsum(counts)
    starts = ends - counts
    first_tile = starts // te
    last_tile = jnp.maximum(ends - 1, 0) // te
    nvis = jnp.where(counts > 0, last_tile - first_tile + 1, 0)
    vis_end = jnp.cumsum(nvis)
    vis_start = vis_end - nvis
    total = vis_end[-1]
    ids = jnp.arange(E, dtype=I32)[None, :]
    v = jnp.arange(nv, dtype=I32)
    vc = jnp.minimum(v, total - 1)
    e = jnp.sum((vis_end[None, :] <= vc[:, None]).astype(I32), axis=1)
    onehot = e[:, None] == ids
    pick = lambda table: jnp.sum(jnp.where(onehot, table[None, :], 0), axis=1)
    tile = (pick(first_tile - vis_start) + vc).astype(I32)
    lo = jnp.maximum(pick(starts), tile * te).astype(I32)
    hi = jnp.minimum(pick(ends), (tile + 1) * te).astype(I32)
    e_next = jnp.min(jnp.where(jnp.logical_and(ids > e[:, None], nvis[None, :] > 0), ids, E), axis=1)
    n_before = jnp.sum(jnp.where(jnp.logical_and(ids < e[:, None], nvis[None, :] > 0), 1, 0), axis=1)
    active = v < total
    first_of_tile = jnp.concatenate([jnp.ones((1,), bool), tile[1:] != tile[:-1]])
    first_of_expert = jnp.concatenate([jnp.ones((1,), bool), e[1:] != e[:-1]])
    flags = jnp.where(active,
                      VISIT_ACTIVE + jnp.where(first_of_tile, VISIT_FIRST_OF_TILE, 0)
                      + jnp.where(first_of_expert, VISIT_FIRST_OF_EXPERT, 0)
                      + jnp.where(e_next < E, VISIT_HAS_NEXT_EXPERT, 0), 0).astype(I32)
    slot = (n_before % 2).astype(I32)
    return tile, e.astype(I32), lo, hi, flags, jnp.minimum(e_next, E - 1).astype(I32), slot


def _combine_kernel(dest_ref, gate_ref, h_ref, wgs_ref, wus_ref, wds_ref, g_ref, b_ref,
                    ys_hbm, o_ref, buf_ref, sem, *, TM):
    def start(n, carry):
        for k in range(TOP_K):
            _row_copy(ys_hbm, dest_ref[k, n], buf_ref.at[k], n, sem).start(priority=k % 2)
        return carry

    def wait(n, carry):
        for k in range(TOP_K):
            _row_copy(ys_hbm, 0, buf_ref.at[k], 0, sem).wait()
        return carry

    lax.fori_loop(0, TM, start, 0)
    h = h_ref[...]
    hb = h.astype(BF16)
    g = jnp.dot(hb, wgs_ref[...], preferred_element_type=F32)
    u = jnp.dot(hb, wus_ref[...], preferred_element_type=F32)
    shared = jnp.dot(((g * _sigmoid(g)) * u).astype(BF16), wds_ref[...], preferred_element_type=F32)
    gate = gate_ref[...].T
    lax.fori_loop(0, TM, wait, 0)
    routed = jnp.zeros(h.shape, F32)
    for k in range(TOP_K):
        routed = routed + buf_ref[k] * gate[:, k:k + 1]
    y = ALPHA * h + (routed + shared)
    o_ref[...] = _layer_norm_rows(y, g_ref[...], b_ref[...])


def combine(dest, gate, h1, w_gate_s, w_up_s, w_down_s, g, b, ys, tm=ROW_TILE):
    T, D = h1.shape
    F = SHARED_FF
    row = pl.BlockSpec((tm, D), lambda i: (i, 0))
    vec = pl.BlockSpec((1, D), lambda i: (0, 0))
    return pl.pallas_call(
        functools.partial(_combine_kernel, TM=tm), grid=(T // tm,),
        in_specs=[_dest_spec(tm), pl.BlockSpec((TOP_K, tm), lambda i: (0, i)), row,
                  pl.BlockSpec((D, F), lambda i: (0, 0)), pl.BlockSpec((D, F), lambda i: (0, 0)),
                  pl.BlockSpec((F, D), lambda i: (0, 0)), vec, vec,
                  pl.BlockSpec(memory_space=pl.ANY)],
        out_specs=row, out_shape=jax.ShapeDtypeStruct((T, D), F32),
        scratch_shapes=[pltpu.VMEM((TOP_K, tm, D), F32), pltpu.SemaphoreType.DMA(())],
        compiler_params=_cparams(), name="moe_combine_ln2")(
            dest, gate, h1, w_gate_s.astype(BF16), w_up_s.astype(BF16), w_down_s.astype(BF16),
            g.reshape(1, D), b.reshape(1, D), ys)


def _mixer(h0f, h0b, w_in, kv_norm_g, w_uk, w_uv, conv_w, conv_b, w_rg_a, b_rg_a, w_rg_x, b_rg_x, rg_lambda,
           w_branch_a, w_branch_b, w_out, ln1_g, ln1_b):
    o_c = Q_DIM
    o_qi = o_c + KV_LATENT
    o_ki = o_qi + IDX_Q_DIM
    o_xr = o_ki + IDX_DIM + IDX_HEADS
    n_gate = 4 * LRU_WIDTH
    w_kw = jnp.pad(w_in[:, o_ki:o_xr], ((0, 0), (0, LANES - IDX_DIM - IDX_HEADS)))
    w_all = jnp.concatenate([w_in[:, o_xr:], w_in[:, :o_c], w_in[:, o_qi:o_ki], w_in[:, o_c:o_qi], w_kw],
                            axis=1).astype(BF16)
    c_qq = n_gate
    c_c = c_qq + Q_DIM + IDX_Q_DIM
    c_kw = c_c + KV_LATENT
    zz = matmul(h0b, w_all, 0, n_gate, F32, 1024, 1024, "proj_lru_gates")
    qq = matmul(h0b, w_all, c_qq, Q_DIM + IDX_Q_DIM, BF16, 1024, 1024, "proj_q")
    c = matmul_rms(h0b, w_all, c_c, KV_LATENT, kv_norm_g)
    kw = matmul(h0b, w_all, c_kw, LANES, F32, 1024, LANES, "proj_idx_k")
    attn = dsa_attention(qq, kw, c, w_uk, w_uv)
    lru = rg_lru(zz, conv_w, conv_b, w_rg_a, b_rg_a, w_rg_x, b_rg_x, rg_lambda)
    merged = merge_branches(attn, lru, w_branch_a.astype(BF16), w_branch_b.astype(BF16), zz)
    return out_proj_ln(merged, w_out.astype(BF16), h0f, ln1_g, ln1_b)


def _moe(h1, w_router, router_bias, w_gate_e, w_up_e, w_down_e, w_gate_s, w_up_s, w_down_s, ln2_g, ln2_b):
    T = h1.shape[0]
    eidx, gate, pos, cnt = router(h1, w_router, router_bias)
    counts = cnt[:, 0].astype(I32)
    starts = jnp.cumsum(counts) - counts
    ids = jnp.arange(N_EXPERTS, dtype=I32)[:, None, None]
    dest = jnp.sum(jnp.where(eidx[None] == ids, starts[:, None, None], 0), axis=0) + pos
    xs = dispatch(dest, h1)
    ys = expert_ffn(visit_tables(counts, T * TOP_K), xs, w_gate_e, w_up_e, w_down_e)
    return combine(dest, gate, h1, w_gate_s, w_up_s, w_down_s, ln2_g, ln2_b, ys)


def kernel(x, ln_in_g, ln_in_b, w_in, kv_norm_g, w_uk, w_uv, conv_w, conv_b, w_rg_a, b_rg_a, w_rg_x, b_rg_x,
           rg_lambda, w_branch_a, w_branch_b, w_out, ln1_g, ln1_b, w_router, router_bias, w_gate_e, w_up_e,
           w_down_e, w_gate_s, w_up_s, w_down_s, ln2_g, ln2_b):
    B, T, D = x.shape
    assert B == 1 and D == D_MODEL and w_in.shape[0] == DEPTH
    hf, hb = ln_in(x.reshape(T, D), ln_in_g, ln_in_b)
    for l in range(DEPTH):
        h1 = _mixer(hf, hb, w_in[l], kv_norm_g[l], w_uk[l], w_uv[l], conv_w[l], conv_b[l], w_rg_a[l], b_rg_a[l],
                    w_rg_x[l], b_rg_x[l], rg_lambda[l], w_branch_a[l], w_branch_b[l], w_out[l], ln1_g[l], ln1_b[l])
        hf = _moe(h1, w_router[l], router_bias[l], w_gate_e[l], w_up_e[l], w_down_e[l], w_gate_s[l], w_up_s[l],
                  w_down_s[l], ln2_g[l], ln2_b[l])
        hb = hf.astype(BF16)
    return hf.reshape(B, T, D)
```

```python
import functools

import jax
import jax.numpy as jnp
from jax import lax
from jax.experimental import pallas as pl
from jax.experimental.pallas import tpu as pltpu

F32 = jnp.float32
BF16 = jnp.bfloat16
I32 = jnp.int32

D_MODEL = 2048
CHUNK = 64
N_HEADS = 16
HEAD_DIM = 128
KV_LATENT = 256
IDX_HEADS = 16
IDX_DIM = 64
TOPK_KEYS_MAX = 256
ATTN_SCALE = HEAD_DIM ** -0.5
LRU_WIDTH = 2048
LRU_BLOCKS = 16
LRU_BLOCK_DIM = LRU_WIDTH // LRU_BLOCKS
CONV_WIDTH = 4
RG_C = 8.0
N_EXPERTS = 64
TOP_K = 8
N_GROUPS = 8
TOPK_GROUPS = 4
EXPERT_FF = 512
SHARED_FF = 512
ROUTE_SCALE = 2.5
LN_EPS = 1e-5
RMS_EPS = 1e-6
Q_DIM = N_HEADS * HEAD_DIM
IDX_Q_DIM = IDX_HEADS * IDX_DIM
DEPTH = 1
ALPHA = (2.0 * DEPTH) ** 0.25
LOG2_E = 1.4426950408889634

VMEM_LIMIT_V7X = 56 * 1024 * 1024
LANES = 128
MXU_WIDTH_V7X = 256

ATTN_QB = 256
ATTN_TK = 512
EXPERT_TILE = 256
ROW_TILE = 128

NT_DIMS = (((1,), (1,)), ((), ()))


def _cparams(n_axes=1, vmem=VMEM_LIMIT_V7X):
    return pltpu.CompilerParams(dimension_semantics=("arbitrary",) * n_axes, vmem_limit_bytes=vmem)


def _layer_norm_rows(x, g, b):
    mu = jnp.mean(x, axis=-1, keepdims=True)
    xc = x - mu
    var = jnp.mean(xc * xc, axis=-1, keepdims=True)
    return xc * lax.rsqrt(var + LN_EPS) * g + b


def _sigmoid(x):
    return 1.0 / (1.0 + jnp.exp(-x))


def _ln_in_kernel(x_ref, g_ref, b_ref, hf_ref, hb_ref):
    y = _layer_norm_rows(x_ref[...], g_ref[...], b_ref[...])
    hf_ref[...] = y
    hb_ref[...] = y.astype(BF16)


def ln_in(x, g, b, tm=512):
    T, D = x.shape
    row = pl.BlockSpec((tm, D), lambda i: (i, 0))
    vec = pl.BlockSpec((1, D), lambda i: (0, 0))
    return pl.pallas_call(
        _ln_in_kernel, grid=(T // tm,), in_specs=[row, vec, vec], out_specs=[row, row],
        out_shape=[jax.ShapeDtypeStruct((T, D), F32), jax.ShapeDtypeStruct((T, D), BF16)],
        compiler_params=_cparams(), name="ln_in")(x, g.reshape(1, D), b.reshape(1, D))


def _mm_kernel(x_ref, w_ref, o_ref):
    o_ref[...] = jnp.dot(x_ref[...], w_ref[...], preferred_element_type=F32).astype(o_ref.dtype)


def matmul(x, w, col0, N, out_dtype, tm, tn, name):
    M, K = x.shape
    assert col0 % tn == 0 and N % tn == 0
    off = col0 // tn
    return pl.pallas_call(
        _mm_kernel, grid=(M // tm, N // tn),
        in_specs=[pl.BlockSpec((tm, K), lambda i, j: (i, 0)), pl.BlockSpec((K, tn), lambda i, j: (0, off + j))],
        out_specs=pl.BlockSpec((tm, tn), lambda i, j: (i, j)),
        out_shape=jax.ShapeDtypeStruct((M, N), out_dtype),
        compiler_params=_cparams(2), name=name)(x, w)


def _mm_rms_kernel(x_ref, w_ref, g_ref, o_ref):
    c = jnp.dot(x_ref[...], w_ref[...], preferred_element_type=F32)
    y = c * lax.rsqrt(jnp.mean(c * c, axis=-1, keepdims=True) + RMS_EPS) * g_ref[...]
    o_ref[...] = y.astype(o_ref.dtype)


def matmul_rms(x, w, col0, N, g, tm=1024):
    M, K = x.shape
    assert col0 % N == 0
    return pl.pallas_call(
        _mm_rms_kernel, grid=(M // tm,),
        in_specs=[pl.BlockSpec((tm, K), lambda i: (i, 0)), pl.BlockSpec((K, N), lambda i: (0, col0 // N)),
                  pl.BlockSpec((1, N), lambda i: (0, 0))],
        out_specs=pl.BlockSpec((tm, N), lambda i: (i, 0)),
        out_shape=jax.ShapeDtypeStruct((M, N), BF16),
        compiler_params=_cparams(), name="proj_kv_rms")(x, w, g.reshape(1, N))


def _attn_kernel(q_ref, qi_ref, wt_ref, kidx_ref, c_ref, ct_ref, wukt_ref, wuv_ref, o_ref,
                 sc_ref, big_ref, bias_ref, acc_ref, m_ref, l_ref, qlt_ref, qis_ref, thr_ref, j_ref,
                 *, QB, TK, NSEL):
    H = N_HEADS
    G = max(1, MXU_WIDTH_V7X // QB)
    NG = H // G
    b = pl.program_id(0)
    lmax = (b + 1) * QB
    nk = lax.div(lmax + (TK - 1), TK)
    ksel = float(NSEL)
    neg_inf = float("-inf")
    fmax = float(jnp.finfo(jnp.float32).max)

    tpos = b * QB + lax.broadcasted_iota(I32, (1, QB), 1)
    limit = (lax.shift_right_logical(tpos, 6) + 1) * CHUNK
    limit_f = limit.astype(F32)

    for h in range(H):
        qis_ref[h * QB:(h + 1) * QB, :] = qi_ref[:, h * IDX_DIM:(h + 1) * IDX_DIM]
        qlat_t = lax.dot_general(wukt_ref[h], q_ref[:, h * HEAD_DIM:(h + 1) * HEAD_DIM], NT_DIMS,
                                 preferred_element_type=F32)
        qlt_ref[h // G, :, (h % G) * QB:(h % G + 1) * QB] = (qlat_t * (ATTN_SCALE * LOG2_E)).astype(BF16)
    wscale = (IDX_HEADS ** -0.5) * (IDX_DIM ** -0.5)
    wrows = [wt_ref[h:h + 1, :] * wscale for h in range(H)]

    RC = 128

    def p1(kt, carry):
        k0 = pl.multiple_of(kt * TK, TK)
        big_ref[...] = lax.dot_general(kidx_ref[pl.ds(k0, TK), :], qis_ref[...], NT_DIMS,
                                       preferred_element_type=F32)
        for r in range(TK // RC):
            rows = slice(r * RC, (r + 1) * RC)
            acc = jnp.zeros((RC, QB), F32)
            for h in range(H):
                acc = acc + wrows[h] * jnp.maximum(big_ref[rows, h * QB:(h + 1) * QB], 0.0)
            key = k0 + r * RC + lax.broadcasted_iota(I32, (RC, QB), 0)
            sc_ref[kt, rows, :] = jnp.where(key < limit, acc, neg_inf)
        return carry

    lax.fori_loop(0, nk, p1, 0)

    CR = 32

    def count_ge(t):
        tb = jnp.broadcast_to(t, (CR, QB))

        def body(kt, cnt):
            hit = jnp.where(sc_ref[kt].reshape(TK // CR, CR, QB) >= tb, 1.0, 0.0)
            return cnt + jnp.sum(hit, axis=0)

        cnt = lax.fori_loop(0, nk, body, jnp.zeros((CR, QB), F32))
        return jnp.sum(cnt, axis=0, keepdims=True)

    def minmax_body(kt, carry):
        mn, mx = carry
        x = sc_ref[kt]
        mx = jnp.maximum(mx, jnp.max(x, axis=0, keepdims=True))
        mn = jnp.minimum(mn, jnp.min(jnp.where(x == neg_inf, fmax, x), axis=0, keepdims=True))
        return mn, mx

    mn, mx = lax.fori_loop(0, nk, minmax_body,
                           (jnp.full((1, QB), fmax, F32), jnp.full((1, QB), neg_inf, F32)))

    c_top = count_ge(mx)
    one = jnp.ones((1, QB), F32)
    zero = jnp.zeros((1, QB), F32)
    all_sel = jnp.where(limit_f <= ksel, one, zero)
    top_tie = (1.0 - all_sel) * jnp.where(c_top >= ksel, one, zero)
    lo0 = jnp.where(top_tie > 0, mx, mn)
    c_hi0 = jnp.where(top_tie > 0, zero, c_top)
    thr0 = jnp.where(all_sel > 0, -fmax, mx)
    tie0 = top_tie * jnp.where(c_top > ksel, one, zero)
    done0 = jnp.maximum(all_sel, top_tie)

    def bis_cond(st):
        return jnp.logical_and(st[0] > 0.5, st[1] < 400)

    def bis_body(st):
        _, it, lo, hi, c_hi, thr, tie, done = st
        n_act = jnp.max(1.0 - done)
        act = done < 0.5
        mid = 0.5 * lo + 0.5 * hi
        adjacent = jnp.logical_or(mid <= lo, mid >= hi)
        c = count_ge(mid)
        ge = c >= ksel
        exact = c == ksel
        stop = jnp.logical_or(exact, adjacent)
        fin_tie = jnp.logical_and(act, jnp.logical_and(adjacent, jnp.logical_not(exact)))
        thr = jnp.where(jnp.logical_and(act, exact), mid, jnp.where(fin_tie, lo, thr))
        tie = jnp.where(fin_tie, one, tie)
        upd = jnp.logical_and(act, jnp.logical_not(stop))
        up_lo = jnp.logical_and(upd, ge)
        up_hi = jnp.logical_and(upd, jnp.logical_not(ge))
        lo = jnp.where(up_lo, mid, lo)
        hi = jnp.where(up_hi, mid, hi)
        c_hi = jnp.where(up_hi, c, c_hi)
        done = jnp.where(jnp.logical_and(act, stop), one, done)
        return n_act, it + 1, lo, hi, c_hi, thr, tie, done

    st = lax.while_loop(bis_cond, bis_body,
                        (jnp.max(1.0 - done0), jnp.int32(0), lo0, mx, c_hi0, thr0, tie0, done0))
    _, _, _, _, c_hi, thr, tie, _ = st
    thr_ref[...] = thr
    big_j = float(2 ** 30)
    j_ref[...] = jnp.full((1, QB), big_j, F32)

    @pl.when(jnp.max(tie) > 0.5)
    def _():
        need = ksel - c_hi

        def count_tie_below(jb):
            def body(kt, cnt):
                keyf = (kt * TK + lax.broadcasted_iota(I32, (TK, QB), 0)).astype(F32)
                hit = jnp.logical_and(sc_ref[kt] == thr, keyf < jb)
                return cnt + jnp.sum(jnp.where(hit, 1.0, 0.0), axis=0, keepdims=True)

            return lax.fori_loop(0, nk, body, jnp.zeros((1, QB), F32))

        def jbody(_, carry):
            jlo, jhi = carry
            jm = jnp.floor((jlo + jhi) * 0.5)
            ok = count_tie_below(jm) >= need
            return jnp.where(ok, jlo, jm), jnp.where(ok, jm, jhi)

        n_steps = max(1, int(sc_ref.shape[0] * TK).bit_length())
        _, jhi = lax.fori_loop(0, n_steps, jbody, (zero, zero + (nk * TK).astype(F32)))
        j_ref[...] = jnp.where(tie > 0.5, jhi, big_j)

    m_ref[...] = jnp.full(m_ref.shape, -1e30, F32)
    l_ref[...] = jnp.zeros(l_ref.shape, F32)
    acc_ref[...] = jnp.zeros(acc_ref.shape, F32)

    def p3(kt, carry):
        k0 = pl.multiple_of(kt * TK, TK)
        ck = c_ref[pl.ds(k0, TK), :]
        ckt = ct_ref[kt]
        x = sc_ref[kt]
        keyf = (k0 + lax.broadcasted_iota(I32, (TK, QB), 0)).astype(F32)
        t = thr_ref[...]
        sel = jnp.logical_or(x > t, jnp.logical_and(x == t, keyf < j_ref[...]))
        bias_ref[...] = jnp.where(sel, 0.0, neg_inf)
        for pr in range(NG):
            s = jnp.dot(ck, qlt_ref[pr], preferred_element_type=F32)
            bias = bias_ref[...]
            if G == 1:
                s = s + bias
            else:
                s = jnp.concatenate([s[:, g * QB:(g + 1) * QB] + bias for g in range(G)], axis=1)
            m_old = m_ref[pr]
            m_new = jnp.maximum(m_old, jnp.max(s, axis=0, keepdims=True))
            p = jnp.exp2(s - m_new)
            alpha = jnp.exp2(m_old - m_new)
            l_ref[pr] = alpha * l_ref[pr] + jnp.sum(p, axis=0, keepdims=True)
            m_ref[pr] = m_new
            pv = jnp.dot(ckt, p.astype(BF16), preferred_element_type=F32)
            acc_ref[pr] = alpha * acc_ref[pr] + pv
        return carry

    lax.fori_loop(0, nk, p3, 0)

    for h in range(H):
        cols = slice((h % G) * QB, (h % G + 1) * QB)
        o_lat_t = acc_ref[h // G, :, cols] / l_ref[h // G, :, cols]
        o_ref[:, h * HEAD_DIM:(h + 1) * HEAD_DIM] = jnp.dot(
            o_lat_t.T.astype(BF16), wuv_ref[h], preferred_element_type=F32).astype(o_ref.dtype)


def dsa_attention(qq, kw, c, w_uk, w_uv, qb=ATTN_QB, tk=ATTN_TK):
    T = c.shape[0]
    tk = min(tk, T)
    nsel = min(TOPK_KEYS_MAX, T // 4)
    assert qb % CHUNK == 0 and T % tk == 0 and T % qb == 0 and tk % qb == 0
    H = N_HEADS
    hg = max(1, MXU_WIDTH_V7X // qb)
    kidx = kw[:, :IDX_DIM].astype(BF16)
    w_t = kw[:, IDX_DIM:IDX_DIM + IDX_HEADS].T
    c_t = c.reshape(T // tk, tk, KV_LATENT).transpose(0, 2, 1)
    kernel = functools.partial(_attn_kernel, QB=qb, TK=tk, NSEL=nsel)
    full = lambda shape: pl.BlockSpec(shape, lambda i: (0,) * len(shape))
    return pl.pallas_call(
        kernel, grid=(T // qb,),
        in_specs=[pl.BlockSpec((qb, Q_DIM), lambda i: (i, 0)),
                  pl.BlockSpec((qb, IDX_Q_DIM), lambda i: (i, Q_DIM // IDX_Q_DIM)),
                  pl.BlockSpec((H, qb), lambda i: (0, i)),
                  full((T, IDX_DIM)), full((T, KV_LATENT)), full((T // tk, KV_LATENT, tk)),
                  full((H, KV_LATENT, HEAD_DIM)), full((H, KV_LATENT, HEAD_DIM))],
        out_specs=pl.BlockSpec((qb, Q_DIM), lambda i: (i, 0)),
        out_shape=jax.ShapeDtypeStruct((T, Q_DIM), BF16),
        scratch_shapes=[pltpu.VMEM((T // tk, tk, qb), F32),
                        pltpu.VMEM((tk, H * qb), F32),
                        pltpu.VMEM((tk, qb), F32),
                        pltpu.VMEM((H // hg, KV_LATENT, hg * qb), F32),
                        pltpu.VMEM((H // hg, 1, hg * qb), F32), pltpu.VMEM((H // hg, 1, hg * qb), F32),
                        pltpu.VMEM((H // hg, KV_LATENT, hg * qb), BF16),
                        pltpu.VMEM((H * qb, IDX_DIM), BF16),
                        pltpu.VMEM((1, qb), F32), pltpu.VMEM((1, qb), F32)],
        compiler_params=_cparams(), name="dsa_attention")(
            qq, qq, w_t, kidx, c, c_t, w_uk.transpose(0, 2, 1).astype(BF16), w_uv.astype(BF16))


def _gelu_tanh(x):
    return 0.5 * x * (1.0 + jnp.tanh(0.7978845608028654 * (x + 0.044715 * (x * x * x))))


def _lru_kernel(xr_ref, yg_ref, cw_ref, cb_ref, wa_ref, ba_ref, wx_ref, bx_ref, lam_ref, o_ref,
                ext_ref, a_ref, u_ref, hs_ref, h_ref, *, TT):
    i = pl.program_id(0)
    C = LRU_WIDTH

    @pl.when(i == 0)
    def _():
        ext_ref[0:8, :] = jnp.zeros((8, C), F32)
        h_ref[...] = jnp.zeros((1, C), F32)

    ext_ref[8:8 + TT, :] = xr_ref[...]
    xc = jnp.zeros((TT, C), F32) + cb_ref[...]
    for j in range(CONV_WIDTH):
        off = 8 - (CONV_WIDTH - 1) + j
        xc = xc + cw_ref[j:j + 1, :] * ext_ref[off:off + TT, :]
    ext_ref[0:8, :] = ext_ref[TT:TT + 8, :]

    lam = lam_ref[...]
    softplus_neg = jnp.maximum(-lam, 0.0) + jnp.log1p(jnp.exp(-jnp.abs(lam)))
    xcb = xc.astype(BF16)
    for n in range(LRU_BLOCKS):
        cols = slice(n * LRU_BLOCK_DIM, (n + 1) * LRU_BLOCK_DIM)
        xb = xcb[:, cols]
        r = _sigmoid(jnp.dot(xb, wa_ref[n], preferred_element_type=F32) + ba_ref[:, cols])
        g = _sigmoid(jnp.dot(xb, wx_ref[n], preferred_element_type=F32) + bx_ref[:, cols])
        log_a = (-RG_C) * r * softplus_neg[:, cols]
        a_ref[:, cols] = jnp.exp(log_a)
        th = jnp.tanh(log_a)
        u_ref[:, cols] = jnp.sqrt((-2.0 * th) / (1.0 - th)) * (g * xc[:, cols])

    def step(t, h):
        h = a_ref[pl.ds(t, 1), :] * h + u_ref[pl.ds(t, 1), :]
        hs_ref[pl.ds(t, 1), :] = h
        return h

    h_ref[...] = lax.fori_loop(0, TT, step, h_ref[...], unroll=8)
    o_ref[...] = (hs_ref[...] * _gelu_tanh(yg_ref[...])).astype(o_ref.dtype)


def rg_lru(zz, conv_w, conv_b, w_a, b_a, w_x, b_x, lam, tt=256):
    T = zz.shape[0]
    tt = min(tt, T)
    C = LRU_WIDTH
    vec = pl.BlockSpec((1, C), lambda i: (0, 0))
    blk = pl.BlockSpec((LRU_BLOCKS, LRU_BLOCK_DIM, LRU_BLOCK_DIM), lambda i: (0, 0, 0))
    return pl.pallas_call(
        functools.partial(_lru_kernel, TT=tt), grid=(T // tt,),
        in_specs=[pl.BlockSpec((tt, C), lambda i: (i, 0)), pl.BlockSpec((tt, C), lambda i: (i, 1)),
                  pl.BlockSpec((CONV_WIDTH, C), lambda i: (0, 0)), vec, blk, vec, blk, vec, vec],
        out_specs=pl.BlockSpec((tt, C), lambda i: (i, 0)),
        out_shape=jax.ShapeDtypeStruct((T, C), BF16),
        scratch_shapes=[pltpu.VMEM((tt + 8, C), F32), pltpu.VMEM((tt, C), F32), pltpu.VMEM((tt, C), F32),
                        pltpu.VMEM((tt, C), F32), pltpu.VMEM((1, C), F32)],
        compiler_params=_cparams(), name="rg_lru")(
            zz, zz, conv_w, conv_b.reshape(1, C), w_a.astype(BF16), b_a.reshape(1, C),
            w_x.astype(BF16), b_x.reshape(1, C), lam.reshape(1, C))


def _merge_kernel(a_ref, l_ref, wa_ref, wb_ref, ga_ref, gb_ref, o_ref):
    ya = jnp.dot(a_ref[...], wa_ref[...], preferred_element_type=F32)
    yb = jnp.dot(l_ref[...], wb_ref[...], preferred_element_type=F32)
    o_ref[...] = (_sigmoid(ga_ref[...]) * ya + _sigmoid(gb_ref[...]) * yb).astype(o_ref.dtype)


def merge_branches(attn, lru, w_a, w_b, zz, tm=1024, tn=512):
    T, D = attn.shape
    nb = D // tn
    return pl.pallas_call(
        _merge_kernel, grid=(T // tm, nb),
        in_specs=[pl.BlockSpec((tm, D), lambda i, j: (i, 0)), pl.BlockSpec((tm, D), lambda i, j: (i, 0)),
                  pl.BlockSpec((D, tn), lambda i, j: (0, j)), pl.BlockSpec((D, tn), lambda i, j: (0, j)),
                  pl.BlockSpec((tm, tn), lambda i, j: (i, 2 * nb + j)),
                  pl.BlockSpec((tm, tn), lambda i, j: (i, 3 * nb + j))],
        out_specs=pl.BlockSpec((tm, tn), lambda i, j: (i, j)),
        out_shape=jax.ShapeDtypeStruct((T, D), BF16),
        compiler_params=_cparams(2), name="merge_branches")(attn, lru, w_a, w_b, zz, zz)


def _post1_kernel(m_ref, w_ref, h_ref, g_ref, b_ref, o_ref):
    y = ALPHA * h_ref[...] + jnp.dot(m_ref[...], w_ref[...], preferred_element_type=F32)
    o_ref[...] = _layer_norm_rows(y, g_ref[...], b_ref[...])


def out_proj_ln(merged, w_out, h0, g, b, tm=512):
    T, D = merged.shape
    row = lambda dt: pl.BlockSpec((tm, D), lambda i: (i, 0))
    vec = pl.BlockSpec((1, D), lambda i: (0, 0))
    return pl.pallas_call(
        _post1_kernel, grid=(T // tm,),
        in_specs=[row(BF16), pl.BlockSpec((D, D), lambda i: (0, 0)), row(F32), vec, vec],
        out_specs=row(F32), out_shape=jax.ShapeDtypeStruct((T, D), F32),
        compiler_params=_cparams(), name="out_proj_ln1")(merged, w_out, h0, g.reshape(1, D), b.reshape(1, D))


def _router_kernel(h_ref, wrt_ref, rb_ref, eidx_ref, gate_ref, pos_ref, cnt_ref, carry_ref, *, TM):
    i = pl.program_id(0)
    E = N_EXPERTS
    per_group = E // N_GROUPS
    neg_inf = float("-inf")

    @pl.when(i == 0)
    def _():
        carry_ref[...] = jnp.zeros((E, 1), F32)

    h = h_ref[...]
    hh = h.astype(BF16)
    hl = (h - hh.astype(F32)).astype(BF16)
    w = wrt_ref[...]
    wh = w.astype(BF16)
    wl = (w - wh.astype(F32)).astype(BF16)
    nt = lambda a, b: lax.dot_general(a, b, NT_DIMS, preferred_element_type=F32)
    logits = nt(wh, hh) + (nt(wl, hh) + nt(wh, hl))
    scores = _sigmoid(logits)
    biased = scores + rb_ref[...]

    b3 = biased.reshape(N_GROUPS, per_group, TM)
    sub = lax.broadcasted_iota(I32, (N_GROUPS, per_group, TM), 1)
    m1 = jnp.max(b3, axis=1, keepdims=True)
    i1 = jnp.min(jnp.where(b3 == m1, sub, per_group), axis=1, keepdims=True)
    m2 = jnp.max(jnp.where(sub == i1, neg_inf, b3), axis=1, keepdims=True)
    gscore = (m1 + m2).reshape(N_GROUPS, TM)
    gid = lax.broadcasted_iota(I32, (N_GROUPS, TM), 0)
    rank = jnp.zeros((N_GROUPS, TM), F32)
    for g2 in range(N_GROUPS):
        other = gscore[g2:g2 + 1, :]
        ahead = jnp.logical_or(other > gscore, jnp.logical_and(other == gscore, gid > g2))
        rank = rank + jnp.where(ahead, 1.0, 0.0)
    keep = jnp.broadcast_to((rank < float(TOPK_GROUPS)).reshape(N_GROUPS, 1, TM), (N_GROUPS, per_group, TM))
    cand = jnp.where(keep, b3, neg_inf).reshape(E, TM)

    eid = lax.broadcasted_iota(I32, (E, TM), 0)
    sel = jnp.zeros((E, TM), F32)
    onehots, gates, picks = [], [], []
    for k in range(TOP_K):
        m = jnp.max(cand, axis=0, keepdims=True)
        ik = jnp.min(jnp.where(cand == m, eid, E), axis=0, keepdims=True)
        oh = eid == ik
        onehots.append(oh)
        picks.append(ik)
        gates.append(jnp.sum(jnp.where(oh, scores, 0.0), axis=0, keepdims=True))
        cand = jnp.where(oh, neg_inf, cand)
        sel = sel + jnp.where(oh, 1.0, 0.0)
    gsum = gates[0]
    for k in range(1, TOP_K):
        gsum = gsum + gates[k]

    r_i = lax.broadcasted_iota(I32, (TM, TM), 0)
    c_i = lax.broadcasted_iota(I32, (TM, TM), 1)
    before = jnp.where(r_i < c_i, 1.0, 0.0).astype(BF16)
    prefix = jnp.dot(sel.astype(BF16), before, preferred_element_type=F32) + carry_ref[...]
    eidx_ref[...] = jnp.concatenate(picks, axis=0)
    gate_ref[...] = jnp.concatenate([gates[k] / gsum * ROUTE_SCALE for k in range(TOP_K)], axis=0)
    pos_ref[...] = jnp.concatenate(
        [jnp.sum(jnp.where(onehots[k], prefix, 0.0), axis=0, keepdims=True) for k in range(TOP_K)],
        axis=0).astype(I32)
    carry_ref[...] = carry_ref[...] + jnp.sum(sel, axis=1, keepdims=True)
    cnt_ref[...] = carry_ref[...]


def router(h1, w_router, router_bias, tm=512):
    T, D = h1.shape
    tm = min(tm, T)
    E = N_EXPERTS
    out8 = pl.BlockSpec((TOP_K, tm), lambda i: (0, i))
    return pl.pallas_call(
        functools.partial(_router_kernel, TM=tm), grid=(T // tm,),
        in_specs=[pl.BlockSpec((tm, D), lambda i: (i, 0)), pl.BlockSpec((E, D), lambda i: (0, 0)),
                  pl.BlockSpec((E, 1), lambda i: (0, 0))],
        out_specs=[out8, out8, out8, pl.BlockSpec((E, 1), lambda i: (0, 0))],
        out_shape=[jax.ShapeDtypeStruct((TOP_K, T), I32), jax.ShapeDtypeStruct((TOP_K, T), F32),
                   jax.ShapeDtypeStruct((TOP_K, T), I32), jax.ShapeDtypeStruct((E, 1), F32)],
        scratch_shapes=[pltpu.VMEM((E, 1), F32)],
        compiler_params=_cparams(), name="router")(h1, w_router.T, router_bias.reshape(E, 1))


def _row_copy(src, src_row, dst, dst_row, sem):
    return pltpu.make_async_copy(src.at[pl.ds(src_row, 1)], dst.at[pl.ds(dst_row, 1)], sem)


def _dispatch_kernel(dest_ref, h_ref, xs_hbm, sem, *, TM):
    def start(n, carry):
        for k in range(TOP_K):
            _row_copy(h_ref, n, xs_hbm, dest_ref[k, n], sem).start(priority=k % 2)
        return carry

    def wait(n, carry):
        for k in range(TOP_K):
            _row_copy(h_ref, 0, xs_hbm, 0, sem).wait()
        return carry

    lax.fori_loop(0, TM, start, 0)
    lax.fori_loop(0, TM, wait, 0)


def _dest_spec(tm):
    return pl.BlockSpec((TOP_K, tm), lambda i: (0, i), memory_space=pltpu.SMEM)


def dispatch(dest, h1, tm=ROW_TILE):
    T, D = h1.shape
    return pl.pallas_call(
        functools.partial(_dispatch_kernel, TM=tm), grid=(T // tm,),
        in_specs=[_dest_spec(tm), pl.BlockSpec((tm, D), lambda i: (i, 0))],
        out_specs=pl.BlockSpec(memory_space=pl.ANY),
        out_shape=jax.ShapeDtypeStruct((T * TOP_K, D), h1.dtype),
        scratch_shapes=[pltpu.SemaphoreType.DMA(())],
        compiler_params=_cparams(), name="moe_dispatch")(dest, h1)


VISIT_FIRST_OF_TILE = 1
VISIT_FIRST_OF_EXPERT = 2
VISIT_ACTIVE = 4
VISIT_HAS_NEXT_EXPERT = 8
X_RING = 3


def _expert_kernel(vt_ref, ve_ref, vlo_ref, vhi_ref, vfl_ref, ven_ref, vsl_ref,
                   xs_hbm, wg_hbm, wu_hbm, wd_hbm, o_ref,
                   xbuf_ref, wgf_ref, wuf_ref, wdf_ref, wgb_ref, wub_ref, wdb_ref, xsem, wsem, *, TE, NT):
    v = pl.program_id(0)
    flags = vfl_ref[v]
    slot = vsl_ref[v]
    tile = vt_ref[v]

    def row_tile_copy(t):
        s = lax.rem(t, X_RING)
        return pltpu.make_async_copy(xs_hbm.at[pl.ds(pl.multiple_of(t * TE, TE), TE)], xbuf_ref.at[s], xsem.at[s])

    def weight_copies(expert, s):
        return (pltpu.make_async_copy(wg_hbm.at[expert], wgf_ref.at[s], wsem.at[0, s]),
                pltpu.make_async_copy(wu_hbm.at[expert], wuf_ref.at[s], wsem.at[1, s]),
                pltpu.make_async_copy(wd_hbm.at[expert], wdf_ref.at[s], wsem.at[2, s]))

    @pl.when(v == 0)
    def _():
        for cp in weight_copies(ve_ref[0], 0):
            cp.start()
        for t in range(min(X_RING - 1, NT)):
            row_tile_copy(t).start()

    @pl.when((flags & VISIT_FIRST_OF_TILE) != 0)
    def _():
        row_tile_copy(tile).wait()

        @pl.when(tile + (X_RING - 1) < NT)
        def _():
            row_tile_copy(tile + (X_RING - 1)).start()

    @pl.when((flags & VISIT_FIRST_OF_EXPERT) != 0)
    def _():
        for cp in weight_copies(ve_ref[v], slot):
            cp.wait()

        @pl.when((flags & VISIT_HAS_NEXT_EXPERT) != 0)
        def _():
            for cp in weight_copies(ven_ref[v], 1 - slot):
                cp.start(priority=1)

        wgb_ref[...] = wgf_ref[slot].astype(BF16)
        wub_ref[...] = wuf_ref[slot].astype(BF16)
        wdb_ref[...] = wdf_ref[slot].astype(BF16)

    @pl.when((flags & VISIT_ACTIVE) != 0)
    def _():
        x = xbuf_ref[lax.rem(tile, X_RING)].astype(BF16)
        g = jnp.dot(x, wgb_ref[...], preferred_element_type=F32)
        u = jnp.dot(x, wub_ref[...], preferred_element_type=F32)
        rows = tile * TE + lax.broadcasted_iota(I32, (TE, 1), 0)
        mine = jnp.logical_and(rows >= vlo_ref[v], rows < vhi_ref[v])
        mid = jnp.where(mine, (g * _sigmoid(g)) * u, 0.0).astype(BF16)
        y = jnp.dot(mid, wdb_ref[...], preferred_element_type=F32)

        @pl.when((flags & VISIT_FIRST_OF_TILE) != 0)
        def _():
            o_ref[...] = y

        @pl.when((flags & VISIT_FIRST_OF_TILE) == 0)
        def _():
            o_ref[...] = o_ref[...] + y


def expert_ffn(tables, xs, w_gate_e, w_up_e, w_down_e, te=EXPERT_TILE):
    R, D = xs.shape
    F = EXPERT_FF
    nv = tables[0].shape[0]
    row_map = lambda v, vt, *_: (vt[v], 0)
    hbm = pl.BlockSpec(memory_space=pl.ANY)
    grid_spec = pltpu.PrefetchScalarGridSpec(
        num_scalar_prefetch=len(tables), grid=(nv,),
        in_specs=[hbm, hbm, hbm, hbm],
        out_specs=pl.BlockSpec((te, D), row_map),
        scratch_shapes=[pltpu.VMEM((X_RING, te, D), F32),
                        pltpu.VMEM((2, D, F), F32), pltpu.VMEM((2, D, F), F32), pltpu.VMEM((2, F, D), F32),
                        pltpu.VMEM((D, F), BF16), pltpu.VMEM((D, F), BF16), pltpu.VMEM((F, D), BF16),
                        pltpu.SemaphoreType.DMA((X_RING,)), pltpu.SemaphoreType.DMA((3, 2))])
    return pl.pallas_call(
        functools.partial(_expert_kernel, TE=te, NT=R // te), grid_spec=grid_spec,
        out_shape=jax.ShapeDtypeStruct((R, D), F32),
        compiler_params=_cparams(), name="expert_ffn")(*tables, xs, w_gate_e, w_up_e, w_down_e)


def visit_tables(counts, n_rows, te=EXPERT_TILE):
    E = counts.shape[0]
    n_tiles = n_rows // te
    nv = n_tiles + E
    ends = jnp.cumsum(counts)
    starts = ends - counts
    first_tile = starts // te
    last_tile = jnp.maximum(ends - 1, 0) // te
    nvis = jnp.where(counts > 0, last_tile - first_tile + 1, 0)
    vis_end = jnp.cumsum(nvis)
    vis_start = vis_end - nvis
    total = vis_end[-1]
    ids = jnp.arange(E, dtype=I32)[None, :]
    v = jnp.arange(nv, dtype=I32)
    vc = jnp.minimum(v, total - 1)
    e = jnp.sum((vis_end[None, :] <= vc[:, None]).astype(I32), axis=1)
    onehot = e[:, None] == ids
    pick = lambda table: jnp.sum(jnp.where(onehot, table[None, :], 0), axis=1)
    tile = (pick(first_tile - vis_start) + vc).astype(I32)
    lo = jnp.maximum(pick(starts), tile * te).astype(I32)
    hi = jnp.minimum(pick(ends), (tile + 1) * te).astype(I32)
    e_next = jnp.min(jnp.where(jnp.logical_and(ids > e[:, None], nvis[None, :] > 0), ids, E), axis=1)
    n_before = jnp.sum(jnp.where(jnp.logical_and(ids < e[:, None], nvis[None, :] > 0), 1, 0), axis=1)
    active = v < total
    first_of_tile = jnp.concatenate([jnp.ones((1,), bool), tile[1:] != tile[:-1]])
    first_of_expert = jnp.concatenate([jnp.ones((1,), bool), e[1:] != e[:-1]])
    flags = jnp.where(active,
                      VISIT_ACTIVE + jnp.where(first_of_tile, VISIT_FIRST_OF_TILE, 0)
                      + jnp.where(first_of_expert, VISIT_FIRST_OF_EXPERT, 0)
                      + jnp.where(e_next < E, VISIT_HAS_NEXT_EXPERT, 0), 0).astype(I32)
    slot = (n_before % 2).astype(I32)
    return tile, e.astype(I32), lo, hi, flags, jnp.minimum(e_next, E - 1).astype(I32), slot


def _combine_kernel(dest_ref, gate_ref, h_ref, wgs_ref, wus_ref, wds_ref, g_ref, b_ref,
                    ys_hbm, o_ref, buf_ref, sem, *, TM):
    def start(n, carry):
        for k in range(TOP_K):
            _row_copy(ys_hbm, dest_ref[k, n], buf_ref.at[k], n, sem).start(priority=k % 2)
        return carry

    def wait(n, carry):
        for k in range(TOP_K):
            _row_copy(ys_hbm, 0, buf_ref.at[k], 0, sem).wait()
        return carry

    lax.fori_loop(0, TM, start, 0)
    h = h_ref[...]
    hb = h.astype(BF16)
    g = jnp.dot(hb, wgs_ref[...], preferred_element_type=F32)
    u = jnp.dot(hb, wus_ref[...], preferred_element_type=F32)
    shared = jnp.dot(((g * _sigmoid(g)) * u).astype(BF16), wds_ref[...], preferred_element_type=F32)
    gate = gate_ref[...].T
    lax.fori_loop(0, TM, wait, 0)
    routed = jnp.zeros(h.shape, F32)
    for k in range(TOP_K):
        routed = routed + buf_ref[k] * gate[:, k:k + 1]
    y = ALPHA * h + (routed + shared)
    o_ref[...] = _layer_norm_rows(y, g_ref[...], b_ref[...])


def combine(dest, gate, h1, w_gate_s, w_up_s, w_down_s, g, b, ys, tm=ROW_TILE):
    T, D = h1.shape
    F = SHARED_FF
    row = pl.BlockSpec((tm, D), lambda i: (i, 0))
    vec = pl.BlockSpec((1, D), lambda i: (0, 0))
    return pl.pallas_call(
        functools.partial(_combine_kernel, TM=tm), grid=(T // tm,),
        in_specs=[_dest_spec(tm), pl.BlockSpec((TOP_K, tm), lambda i: (0, i)), row,
                  pl.BlockSpec((D, F), lambda i: (0, 0)), pl.BlockSpec((D, F), lambda i: (0, 0)),
                  pl.BlockSpec((F, D), lambda i: (0, 0)), vec, vec,
                  pl.BlockSpec(memory_space=pl.ANY)],
        out_specs=row, out_shape=jax.ShapeDtypeStruct((T, D), F32),
        scratch_shapes=[pltpu.VMEM((TOP_K, tm, D), F32), pltpu.SemaphoreType.DMA(())],
        compiler_params=_cparams(), name="moe_combine_ln2")(
            dest, gate, h1, w_gate_s.astype(BF16), w_up_s.astype(BF16), w_down_s.astype(BF16),
            g.reshape(1, D), b.reshape(1, D), ys)


def _mixer(h0f, h0b, w_in, kv_norm_g, w_uk, w_uv, conv_w, conv_b, w_rg_a, b_rg_a, w_rg_x, b_rg_x, rg_lambda,
           w_branch_a, w_branch_b, w_out, ln1_g, ln1_b):
    o_c = Q_DIM
    o_qi = o_c + KV_LATENT
    o_ki = o_qi + IDX_Q_DIM
    o_xr = o_ki + IDX_DIM + IDX_HEADS
    n_gate = 4 * LRU_WIDTH
    w_kw = jnp.pad(w_in[:, o_ki:o_xr], ((0, 0), (0, LANES - IDX_DIM - IDX_HEADS)))
    w_all = jnp.concatenate([w_in[:, o_xr:], w_in[:, :o_c], w_in[:, o_qi:o_ki], w_in[:, o_c:o_qi], w_kw],
                            axis=1).astype(BF16)
    c_qq = n_gate
    c_c = c_qq + Q_DIM + IDX_Q_DIM
    c_kw = c_c + KV_LATENT
    zz = matmul(h0b, w_all, 0, n_gate, F32, 1024, 1024, "proj_lru_gates")
    qq = matmul(h0b, w_all, c_qq, Q_DIM + IDX_Q_DIM, BF16, 1024, 1024, "proj_q")
    c = matmul_rms(h0b, w_all, c_c, KV_LATENT, kv_norm_g)
    kw = matmul(h0b, w_all, c_kw, LANES, F32, 1024, LANES, "proj_idx_k")
    attn = dsa_attention(qq, kw, c, w_uk, w_uv)
    lru = rg_lru(zz, conv_w, conv_b, w_rg_a, b_rg_a, w_rg_x, b_rg_x, rg_lambda)
    merged = merge_branches(attn, lru, w_branch_a.astype(BF16), w_branch_b.astype(BF16), zz)
    return out_proj_ln(merged, w_out.astype(BF16), h0f, ln1_g, ln1_b)


def _moe(h1, w_router, router_bias, w_gate_e, w_up_e, w_down_e, w_gate_s, w_up_s, w_down_s, ln2_g, ln2_b):
    T = h1.shape[0]
    eidx, gate, pos, cnt = router(h1, w_router, router_bias)
    counts = cnt[:, 0].astype(I32)
    starts = jnp.cumsum(counts) - counts
    ids = jnp.arange(N_EXPERTS, dtype=I32)[:, None, None]
    dest = jnp.sum(jnp.where(eidx[None] == ids, starts[:, None, None], 0), axis=0) + pos
    xs = dispatch(dest, h1)
    ys = expert_ffn(visit_tables(counts, T * TOP_K), xs, w_gate_e, w_up_e, w_down_e)
    return combine(dest, gate, h1, w_gate_s, w_up_s, w_down_s, ln2_g, ln2_b, ys)


def kernel(x, ln_in_g, ln_in_b, w_in, kv_norm_g, w_uk, w_uv, conv_w, conv_b, w_rg_a, b_rg_a, w_rg_x, b_rg_x,
           rg_lambda, w_branch_a, w_branch_b, w_out, ln1_g, ln1_b, w_router, router_bias, w_gate_e, w_up_e,
           w_down_e, w_gate_s, w_up_s, w_down_s, ln2_g, ln2_b):
    B, T, D = x.shape
    assert B == 1 and D == D_MODEL and w_in.shape[0] == DEPTH
    hf, hb = ln_in(x.reshape(T, D), ln_in_g, ln_in_b)
    for l in range(DEPTH):
        h1 = _mixer(hf, hb, w_in[l], kv_norm_g[l], w_uk[l], w_uv[l], conv_w[l], conv_b[l], w_rg_a[l], b_rg_a[l],
                    w_rg_x[l], b_rg_x[l], rg_lambda[l], w_branch_a[l], w_branch_b[l], w_out[l], ln1_g[l], ln1_b[l])
        hf = _moe(h1, w_router[l], router_bias[l], w_gate_e[l], w_up_e[l], w_down_e[l], w_gate_s[l], w_up_s[l],
                  w_down_s[l], ln2_g[l], ln2_b[l])
        hb = hf.astype(BF16)
    return hf.reshape(B, T, D)
```

```python
import functools

import jax
import jax.numpy as jnp
from jax import lax
from jax.experimental import pallas as pl
from jax.experimental.pallas import tpu as pltpu

F32 = jnp.float32
BF16 = jnp.bfloat16
I32 = jnp.int32

D_MODEL = 2048
CHUNK = 64
N_HEADS = 16
HEAD_DIM = 128
KV_LATENT = 256
IDX_HEADS = 16
IDX_DIM = 64
TOPK_KEYS_MAX = 256
ATTN_SCALE = HEAD_DIM ** -0.5
LRU_WIDTH = 2048
LRU_BLOCKS = 16
LRU_BLOCK_DIM = LRU_WIDTH // LRU_BLOCKS
CONV_WIDTH = 4
RG_C = 8.0
N_EXPERTS = 64
TOP_K = 8
N_GROUPS = 8
TOPK_GROUPS = 4
EXPERT_FF = 512
SHARED_FF = 512
ROUTE_SCALE = 2.5
LN_EPS = 1e-5
RMS_EPS = 1e-6
Q_DIM = N_HEADS * HEAD_DIM
IDX_Q_DIM = IDX_HEADS * IDX_DIM
DEPTH = 1
ALPHA = (2.0 * DEPTH) ** 0.25
LOG2_E = 1.4426950408889634

VMEM_LIMIT_V7X = 56 * 1024 * 1024
LANES = 128
MXU_WIDTH_V7X = 256

ATTN_QB = 256
ATTN_TK = 512
BISECT_STEPS_PER_TEST = 2
EXPERT_TILE = 256
ROW_TILE = 128

NT_DIMS = (((1,), (1,)), ((), ()))


def _cparams(n_axes=1, vmem=VMEM_LIMIT_V7X):
    return pltpu.CompilerParams(dimension_semantics=("arbitrary",) * n_axes, vmem_limit_bytes=vmem)


def _layer_norm_rows(x, g, b):
    mu = jnp.mean(x, axis=-1, keepdims=True)
    xc = x - mu
    var = jnp.mean(xc * xc, axis=-1, keepdims=True)
    return xc * lax.rsqrt(var + LN_EPS) * g + b


def _sigmoid(x):
    return 1.0 / (1.0 + jnp.exp(-x))


def _ln_in_kernel(x_ref, g_ref, b_ref, hf_ref, hb_ref):
    y = _layer_norm_rows(x_ref[...], g_ref[...], b_ref[...])
    hf_ref[...] = y
    hb_ref[...] = y.astype(BF16)


def ln_in(x, g, b, tm=512):
    T, D = x.shape
    row = pl.BlockSpec((tm, D), lambda i: (i, 0))
    vec = pl.BlockSpec((1, D), lambda i: (0, 0))
    return pl.pallas_call(
        _ln_in_kernel, grid=(T // tm,), in_specs=[row, vec, vec], out_specs=[row, row],
        out_shape=[jax.ShapeDtypeStruct((T, D), F32), jax.ShapeDtypeStruct((T, D), BF16)],
        compiler_params=_cparams(), name="ln_in")(x, g.reshape(1, D), b.reshape(1, D))


def _mm_kernel(x_ref, w_ref, o_ref):
    o_ref[...] = jnp.dot(x_ref[...], w_ref[...], preferred_element_type=F32).astype(o_ref.dtype)


def matmul(x, w, col0, N, out_dtype, tm, tn, name):
    M, K = x.shape
    assert col0 % tn == 0 and N % tn == 0
    off = col0 // tn
    return pl.pallas_call(
        _mm_kernel, grid=(M // tm, N // tn),
        in_specs=[pl.BlockSpec((tm, K), lambda i, j: (i, 0)), pl.BlockSpec((K, tn), lambda i, j: (0, off + j))],
        out_specs=pl.BlockSpec((tm, tn), lambda i, j: (i, j)),
        out_shape=jax.ShapeDtypeStruct((M, N), out_dtype),
        compiler_params=_cparams(2), name=name)(x, w)


def _mm_kv_idx_kernel(x_ref, wc_ref, wk_ref, g_ref, c_ref, kw_ref):
    x = x_ref[...]
    c = jnp.dot(x, wc_ref[...], preferred_element_type=F32)
    y = c * lax.rsqrt(jnp.mean(c * c, axis=-1, keepdims=True) + RMS_EPS) * g_ref[...]
    c_ref[...] = y.astype(c_ref.dtype)
    kw_ref[...] = jnp.dot(x, wk_ref[...], preferred_element_type=F32)


def proj_kv_idx(x, w, col_c, col_kw, g, tm=1024):
    M, K = x.shape
    assert col_c % KV_LATENT == 0 and col_kw % LANES == 0
    return pl.pallas_call(
        _mm_kv_idx_kernel, grid=(M // tm,),
        in_specs=[pl.BlockSpec((tm, K), lambda i: (i, 0)),
                  pl.BlockSpec((K, KV_LATENT), lambda i: (0, col_c // KV_LATENT)),
                  pl.BlockSpec((K, LANES), lambda i: (0, col_kw // LANES)),
                  pl.BlockSpec((1, KV_LATENT), lambda i: (0, 0))],
        out_specs=[pl.BlockSpec((tm, KV_LATENT), lambda i: (i, 0)), pl.BlockSpec((tm, LANES), lambda i: (i, 0))],
        out_shape=[jax.ShapeDtypeStruct((M, KV_LATENT), BF16), jax.ShapeDtypeStruct((M, LANES), F32)],
        compiler_params=_cparams(), name="proj_kv_idx")(x, w, w, g.reshape(1, KV_LATENT))


def _attn_kernel(q_ref, qi_ref, wt_ref, kidx_ref, c_ref, ct_ref, wukt_ref, wuv_ref, o_ref,
                 sc_ref, big_ref, bias_ref, acc_ref, m_ref, l_ref, qlt_ref, qis_ref, thr_ref, j_ref,
                 *, QB, TK, NSEL):
    H = N_HEADS
    G = max(1, MXU_WIDTH_V7X // QB)
    NG = H // G
    b = pl.program_id(0)
    lmax = (b + 1) * QB
    nk = lax.div(lmax + (TK - 1), TK)
    ksel = float(NSEL)
    neg_inf = float("-inf")
    fmax = float(jnp.finfo(jnp.float32).max)

    tpos = b * QB + lax.broadcasted_iota(I32, (1, QB), 1)
    limit = (lax.shift_right_logical(tpos, 6) + 1) * CHUNK
    limit_f = limit.astype(F32)

    for h in range(H):
        qis_ref[h * QB:(h + 1) * QB, :] = qi_ref[:, h * IDX_DIM:(h + 1) * IDX_DIM]
        qlat_t = lax.dot_general(wukt_ref[h], q_ref[:, h * HEAD_DIM:(h + 1) * HEAD_DIM], NT_DIMS,
                                 preferred_element_type=F32)
        qlt_ref[h // G, :, (h % G) * QB:(h % G + 1) * QB] = (qlat_t * (ATTN_SCALE * LOG2_E)).astype(BF16)
    wscale = (IDX_HEADS ** -0.5) * (IDX_DIM ** -0.5)
    wrows = [wt_ref[h:h + 1, :] * wscale for h in range(H)]

    RC = 128

    def p1(kt, carry):
        k0 = pl.multiple_of(kt * TK, TK)
        big_ref[...] = lax.dot_general(kidx_ref[pl.ds(k0, TK), :], qis_ref[...], NT_DIMS,
                                       preferred_element_type=F32)
        for r in range(TK // RC):
            rows = slice(r * RC, (r + 1) * RC)
            acc = jnp.zeros((RC, QB), F32)
            for h in range(H):
                acc = acc + wrows[h] * jnp.maximum(big_ref[rows, h * QB:(h + 1) * QB], 0.0)
            key = k0 + r * RC + lax.broadcasted_iota(I32, (RC, QB), 0)
            sc_ref[kt, rows, :] = jnp.where(key < limit, acc, neg_inf)
        return carry

    lax.fori_loop(0, nk, p1, 0)

    CR = 32

    def count_ge(t):
        tb = jnp.broadcast_to(t, (CR, QB))

        def body(kt, cnt):
            hit = jnp.where(sc_ref[kt].reshape(TK // CR, CR, QB) >= tb, 1.0, 0.0)
            return cnt + jnp.sum(hit, axis=0)

        cnt = lax.fori_loop(0, nk, body, jnp.zeros((CR, QB), F32))
        return jnp.sum(cnt, axis=0, keepdims=True)

    def minmax_body(kt, carry):
        mn, mx = carry
        x = sc_ref[kt]
        mx = jnp.maximum(mx, jnp.max(x, axis=0, keepdims=True))
        mn = jnp.minimum(mn, jnp.min(jnp.where(x == neg_inf, fmax, x), axis=0, keepdims=True))
        return mn, mx

    mn, mx = lax.fori_loop(0, nk, minmax_body,
                           (jnp.full((1, QB), fmax, F32), jnp.full((1, QB), neg_inf, F32)))

    c_top = count_ge(mx)
    one = jnp.ones((1, QB), F32)
    zero = jnp.zeros((1, QB), F32)
    all_sel = jnp.where(limit_f <= ksel, one, zero)
    top_tie = (1.0 - all_sel) * jnp.where(c_top >= ksel, one, zero)
    lo0 = jnp.where(top_tie > 0, mx, mn)
    c_hi0 = jnp.where(top_tie > 0, zero, c_top)
    thr0 = jnp.where(all_sel > 0, -fmax, mx)
    tie0 = top_tie * jnp.where(c_top > ksel, one, zero)
    done0 = jnp.maximum(all_sel, top_tie)

    def bis_cond(st):
        return jnp.logical_and(st[0] > 0.5, st[1] < 400)

    def bis_body(st):
        n_act = jnp.max(1.0 - st[-1])
        inner = st[1:]
        for _ in range(BISECT_STEPS_PER_TEST):
            inner = bis_step(inner)
        return (n_act,) + inner

    def bis_step(st):
        it, lo, hi, c_hi, thr, tie, done = st
        act = done < 0.5
        mid = 0.5 * lo + 0.5 * hi
        adjacent = jnp.logical_or(mid <= lo, mid >= hi)
        c = count_ge(mid)
        ge = c >= ksel
        exact = c == ksel
        stop = jnp.logical_or(exact, adjacent)
        fin_tie = jnp.logical_and(act, jnp.logical_and(adjacent, jnp.logical_not(exact)))
        thr = jnp.where(jnp.logical_and(act, exact), mid, jnp.where(fin_tie, lo, thr))
        tie = jnp.where(fin_tie, one, tie)
        upd = jnp.logical_and(act, jnp.logical_not(stop))
        up_lo = jnp.logical_and(upd, ge)
        up_hi = jnp.logical_and(upd, jnp.logical_not(ge))
        lo = jnp.where(up_lo, mid, lo)
        hi = jnp.where(up_hi, mid, hi)
        c_hi = jnp.where(up_hi, c, c_hi)
        done = jnp.where(jnp.logical_and(act, stop), one, done)
        return it + 1, lo, hi, c_hi, thr, tie, done

    st = lax.while_loop(bis_cond, bis_body,
                        (jnp.max(1.0 - done0), jnp.int32(0), lo0, mx, c_hi0, thr0, tie0, done0))
    _, _, _, _, c_hi, thr, tie, _ = st
    thr_ref[...] = thr
    big_j = float(2 ** 30)
    j_ref[...] = jnp.full((1, QB), big_j, F32)

    @pl.when(jnp.max(tie) > 0.5)
    def _():
        need = ksel - c_hi

        def count_tie_below(jb):
            def body(kt, cnt):
                keyf = (kt * TK + lax.broadcasted_iota(I32, (TK, QB), 0)).astype(F32)
                hit = jnp.logical_and(sc_ref[kt] == thr, keyf < jb)
                return cnt + jnp.sum(jnp.where(hit, 1.0, 0.0), axis=0, keepdims=True)

            return lax.fori_loop(0, nk, body, jnp.zeros((1, QB), F32))

        def jbody(_, carry):
            jlo, jhi = carry
            jm = jnp.floor((jlo + jhi) * 0.5)
            ok = count_tie_below(jm) >= need
            return jnp.where(ok, jlo, jm), jnp.where(ok, jm, jhi)

        n_steps = max(1, int(sc_ref.shape[0] * TK).bit_length())
        _, jhi = lax.fori_loop(0, n_steps, jbody, (zero, zero + (nk * TK).astype(F32)))
        j_ref[...] = jnp.where(tie > 0.5, jhi, big_j)

    m_ref[...] = jnp.full(m_ref.shape, -1e30, F32)
    l_ref[...] = jnp.zeros(l_ref.shape, F32)
    acc_ref[...] = jnp.zeros(acc_ref.shape, F32)

    def p3(kt, carry):
        k0 = pl.multiple_of(kt * TK, TK)
        ck = c_ref[pl.ds(k0, TK), :]
        ckt = ct_ref[kt]
        x = sc_ref[kt]
        keyf = (k0 + lax.broadcasted_iota(I32, (TK, QB), 0)).astype(F32)
        t = thr_ref[...]
        sel = jnp.logical_or(x > t, jnp.logical_and(x == t, keyf < j_ref[...]))
        bias_ref[...] = jnp.where(sel, 0.0, neg_inf)
        for pr in range(NG):
            s = jnp.dot(ck, qlt_ref[pr], preferred_element_type=F32)
            bias = bias_ref[...]
            if G == 1:
                s = s + bias
            else:
                s = jnp.concatenate([s[:, g * QB:(g + 1) * QB] + bias for g in range(G)], axis=1)
            m_old = m_ref[pr]
            m_new = jnp.maximum(m_old, jnp.max(s, axis=0, keepdims=True))
            p = jnp.exp2(s - m_new)
            alpha = jnp.exp2(m_old - m_new)
            l_ref[pr] = alpha * l_ref[pr] + jnp.sum(p, axis=0, keepdims=True)
            m_ref[pr] = m_new
            pv = jnp.dot(ckt, p.astype(BF16), preferred_element_type=F32)
            acc_ref[pr] = alpha * acc_ref[pr] + pv
        return carry

    lax.fori_loop(0, nk, p3, 0)

    for h in range(H):
        cols = slice((h % G) * QB, (h % G + 1) * QB)
        o_lat_t = acc_ref[h // G, :, cols] / l_ref[h // G, :, cols]
        o_ref[:, h * HEAD_DIM:(h + 1) * HEAD_DIM] = jnp.dot(
            o_lat_t.T.astype(BF16), wuv_ref[h], preferred_element_type=F32).astype(o_ref.dtype)


def dsa_attention(qq, kw, c, w_uk, w_uv, qb=ATTN_QB, tk=ATTN_TK):
    T = c.shape[0]
    tk = min(tk, T)
    nsel = min(TOPK_KEYS_MAX, T // 4)
    assert qb % CHUNK == 0 and T % tk == 0 and T % qb == 0 and tk % qb == 0
    H = N_HEADS
    hg = max(1, MXU_WIDTH_V7X // qb)
    kidx = kw[:, :IDX_DIM].astype(BF16)
    w_t = kw[:, IDX_DIM:IDX_DIM + IDX_HEADS].T
    c_t = c.reshape(T // tk, tk, KV_LATENT).transpose(0, 2, 1)
    kernel = functools.partial(_attn_kernel, QB=qb, TK=tk, NSEL=nsel)
    full = lambda shape: pl.BlockSpec(shape, lambda i: (0,) * len(shape))
    return pl.pallas_call(
        kernel, grid=(T // qb,),
        in_specs=[pl.BlockSpec((qb, Q_DIM), lambda i: (i, 0)),
                  pl.BlockSpec((qb, IDX_Q_DIM), lambda i: (i, Q_DIM // IDX_Q_DIM)),
                  pl.BlockSpec((H, qb), lambda i: (0, i)),
                  full((T, IDX_DIM)), full((T, KV_LATENT)), full((T // tk, KV_LATENT, tk)),
                  full((H, KV_LATENT, HEAD_DIM)), full((H, KV_LATENT, HEAD_DIM))],
        out_specs=pl.BlockSpec((qb, Q_DIM), lambda i: (i, 0)),
        out_shape=jax.ShapeDtypeStruct((T, Q_DIM), BF16),
        scratch_shapes=[pltpu.VMEM((T // tk, tk, qb), F32),
                        pltpu.VMEM((tk, H * qb), F32),
                        pltpu.VMEM((tk, qb), F32),
                        pltpu.VMEM((H // hg, KV_LATENT, hg * qb), F32),
                        pltpu.VMEM((H // hg, 1, hg * qb), F32), pltpu.VMEM((H // hg, 1, hg * qb), F32),
                        pltpu.VMEM((H // hg, KV_LATENT, hg * qb), BF16),
                        pltpu.VMEM((H * qb, IDX_DIM), BF16),
                        pltpu.VMEM((1, qb), F32), pltpu.VMEM((1, qb), F32)],
        compiler_params=_cparams(), name="dsa_attention")(
            qq, qq, w_t, kidx, c, c_t, w_uk.transpose(0, 2, 1).astype(BF16), w_uv.astype(BF16))


def _gelu_tanh(x):
    return 0.5 * x * (1.0 + jnp.tanh(0.7978845608028654 * (x + 0.044715 * (x * x * x))))


def _lru_kernel(xr_ref, yg_ref, cw_ref, cb_ref, wa_ref, ba_ref, wx_ref, bx_ref, lam_ref, o_ref,
                ext_ref, a_ref, u_ref, hs_ref, h_ref, *, TT):
    i = pl.program_id(0)
    C = LRU_WIDTH

    @pl.when(i == 0)
    def _():
        ext_ref[0:8, :] = jnp.zeros((8, C), F32)
        h_ref[...] = jnp.zeros((1, C), F32)

    ext_ref[8:8 + TT, :] = xr_ref[...]
    xc = jnp.zeros((TT, C), F32) + cb_ref[...]
    for j in range(CONV_WIDTH):
        off = 8 - (CONV_WIDTH - 1) + j
        xc = xc + cw_ref[j:j + 1, :] * ext_ref[off:off + TT, :]
    ext_ref[0:8, :] = ext_ref[TT:TT + 8, :]

    lam = lam_ref[...]
    softplus_neg = jnp.maximum(-lam, 0.0) + jnp.log1p(jnp.exp(-jnp.abs(lam)))
    xcb = xc.astype(BF16)
    for n in range(LRU_BLOCKS):
        cols = slice(n * LRU_BLOCK_DIM, (n + 1) * LRU_BLOCK_DIM)
        xb = xcb[:, cols]
        r = _sigmoid(jnp.dot(xb, wa_ref[n], preferred_element_type=F32) + ba_ref[:, cols])
        g = _sigmoid(jnp.dot(xb, wx_ref[n], preferred_element_type=F32) + bx_ref[:, cols])
        log_a = (-RG_C) * r * softplus_neg[:, cols]
        a_ref[:, cols] = jnp.exp(log_a)
        th = jnp.tanh(log_a)
        u_ref[:, cols] = jnp.sqrt((-2.0 * th) / (1.0 - th)) * (g * xc[:, cols])

    def step(t, h):
        h = a_ref[pl.ds(t, 1), :] * h + u_ref[pl.ds(t, 1), :]
        hs_ref[pl.ds(t, 1), :] = h
        return h

    h_ref[...] = lax.fori_loop(0, TT, step, h_ref[...], unroll=8)
    o_ref[...] = (hs_ref[...] * _gelu_tanh(yg_ref[...])).astype(o_ref.dtype)


def rg_lru(zz, conv_w, conv_b, w_a, b_a, w_x, b_x, lam, tt=256):
    T = zz.shape[0]
    tt = min(tt, T)
    C = LRU_WIDTH
    vec = pl.BlockSpec((1, C), lambda i: (0, 0))
    blk = pl.BlockSpec((LRU_BLOCKS, LRU_BLOCK_DIM, LRU_BLOCK_DIM), lambda i: (0, 0, 0))
    return pl.pallas_call(
        functools.partial(_lru_kernel, TT=tt), grid=(T // tt,),
        in_specs=[pl.BlockSpec((tt, C), lambda i: (i, 0)), pl.BlockSpec((tt, C), lambda i: (i, 1)),
                  pl.BlockSpec((CONV_WIDTH, C), lambda i: (0, 0)), vec, blk, vec, blk, vec, vec],
        out_specs=pl.BlockSpec((tt, C), lambda i: (i, 0)),
        out_shape=jax.ShapeDtypeStruct((T, C), BF16),
        scratch_shapes=[pltpu.VMEM((tt + 8, C), F32), pltpu.VMEM((tt, C), F32), pltpu.VMEM((tt, C), F32),
                        pltpu.VMEM((tt, C), F32), pltpu.VMEM((1, C), F32)],
        compiler_params=_cparams(), name="rg_lru")(
            zz, zz, conv_w, conv_b.reshape(1, C), w_a.astype(BF16), b_a.reshape(1, C),
            w_x.astype(BF16), b_x.reshape(1, C), lam.reshape(1, C))


def _merge_kernel(a_ref, l_ref, wa_ref, wb_ref, ga_ref, gb_ref, o_ref):
    ya = jnp.dot(a_ref[...], wa_ref[...], preferred_element_type=F32)
    yb = jnp.dot(l_ref[...], wb_ref[...], preferred_element_type=F32)
    o_ref[...] = (_sigmoid(ga_ref[...]) * ya + _sigmoid(gb_ref[...]) * yb).astype(o_ref.dtype)


def merge_branches(attn, lru, w_a, w_b, zz, tm=1024, tn=512):
    T, D = attn.shape
    nb = D // tn
    return pl.pallas_call(
        _merge_kernel, grid=(T // tm, nb),
        in_specs=[pl.BlockSpec((tm, D), lambda i, j: (i, 0)), pl.BlockSpec((tm, D), lambda i, j: (i, 0)),
                  pl.BlockSpec((D, tn), lambda i, j: (0, j)), pl.BlockSpec((D, tn), lambda i, j: (0, j)),
                  pl.BlockSpec((tm, tn), lambda i, j: (i, 2 * nb + j)),
                  pl.BlockSpec((tm, tn), lambda i, j: (i, 3 * nb + j))],
        out_specs=pl.BlockSpec((tm, tn), lambda i, j: (i, j)),
        out_shape=jax.ShapeDtypeStruct((T, D), BF16),
        compiler_params=_cparams(2), name="merge_branches")(attn, lru, w_a, w_b, zz, zz)


def _post1_kernel(m_ref, w_ref, h_ref, g_ref, b_ref, o_ref):
    y = ALPHA * h_ref[...] + jnp.dot(m_ref[...], w_ref[...], preferred_element_type=F32)
    o_ref[...] = _layer_norm_rows(y, g_ref[...], b_ref[...])


def out_proj_ln(merged, w_out, h0, g, b, tm=512):
    T, D = merged.shape
    row = lambda dt: pl.BlockSpec((tm, D), lambda i: (i, 0))
    vec = pl.BlockSpec((1, D), lambda i: (0, 0))
    return pl.pallas_call(
        _post1_kernel, grid=(T // tm,),
        in_specs=[row(BF16), pl.BlockSpec((D, D), lambda i: (0, 0)), row(F32), vec, vec],
        out_specs=row(F32), out_shape=jax.ShapeDtypeStruct((T, D), F32),
        compiler_params=_cparams(), name="out_proj_ln1")(merged, w_out, h0, g.reshape(1, D), b.reshape(1, D))


def _router_kernel(h_ref, wrt_ref, rb_ref, eidx_ref, gate_ref, pos_ref, cnt_ref, carry_ref, *, TM):
    i = pl.program_id(0)
    E = N_EXPERTS
    per_group = E // N_GROUPS
    neg_inf = float("-inf")

    @pl.when(i == 0)
    def _():
        carry_ref[...] = jnp.zeros((E, 1), F32)

    h = h_ref[...]
    hh = h.astype(BF16)
    hl = (h - hh.astype(F32)).astype(BF16)
    w = wrt_ref[...]
    wh = w.astype(BF16)
    wl = (w - wh.astype(F32)).astype(BF16)
    nt = lambda a, b: lax.dot_general(a, b, NT_DIMS, preferred_element_type=F32)
    logits = nt(wh, hh) + (nt(wl, hh) + nt(wh, hl))
    scores = _sigmoid(logits)
    biased = scores + rb_ref[...]

    b3 = biased.reshape(N_GROUPS, per_group, TM)
    sub = lax.broadcasted_iota(I32, (N_GROUPS, per_group, TM), 1)
    m1 = jnp.max(b3, axis=1, keepdims=True)
    i1 = jnp.min(jnp.where(b3 == m1, sub, per_group), axis=1, keepdims=True)
    m2 = jnp.max(jnp.where(sub == i1, neg_inf, b3), axis=1, keepdims=True)
    gscore = (m1 + m2).reshape(N_GROUPS, TM)
    gid = lax.broadcasted_iota(I32, (N_GROUPS, TM), 0)
    rank = jnp.zeros((N_GROUPS, TM), F32)
    for g2 in range(N_GROUPS):
        other = gscore[g2:g2 + 1, :]
        ahead = jnp.logical_or(other > gscore, jnp.logical_and(other == gscore, gid > g2))
        rank = rank + jnp.where(ahead, 1.0, 0.0)
    keep = jnp.broadcast_to((rank < float(TOPK_GROUPS)).reshape(N_GROUPS, 1, TM), (N_GROUPS, per_group, TM))
    cand = jnp.where(keep, b3, neg_inf).reshape(E, TM)

    eid = lax.broadcasted_iota(I32, (E, TM), 0)
    sel = jnp.zeros((E, TM), F32)
    onehots, gates, picks = [], [], []
    for k in range(TOP_K):
        m = jnp.max(cand, axis=0, keepdims=True)
        ik = jnp.min(jnp.where(cand == m, eid, E), axis=0, keepdims=True)
        oh = eid == ik
        onehots.append(oh)
        picks.append(ik)
        gates.append(jnp.sum(jnp.where(oh, scores, 0.0), axis=0, keepdims=True))
        cand = jnp.where(oh, neg_inf, cand)
        sel = sel + jnp.where(oh, 1.0, 0.0)
    gsum = gates[0]
    for k in range(1, TOP_K):
        gsum = gsum + gates[k]

    r_i = lax.broadcasted_iota(I32, (TM, TM), 0)
    c_i = lax.broadcasted_iota(I32, (TM, TM), 1)
    before = jnp.where(r_i < c_i, 1.0, 0.0).astype(BF16)
    prefix = jnp.dot(sel.astype(BF16), before, preferred_element_type=F32) + carry_ref[...]
    eidx_ref[...] = jnp.concatenate(picks, axis=0)
    gate_ref[...] = jnp.concatenate([gates[k] / gsum * ROUTE_SCALE for k in range(TOP_K)], axis=0)
    pos_ref[...] = jnp.concatenate(
        [jnp.sum(jnp.where(onehots[k], prefix, 0.0), axis=0, keepdims=True) for k in range(TOP_K)],
        axis=0).astype(I32)
    carry_ref[...] = carry_ref[...] + jnp.sum(sel, axis=1, keepdims=True)
    cnt_ref[...] = carry_ref[...]


def router(h1, w_router, router_bias, tm=512):
    T, D = h1.shape
    tm = min(tm, T)
    E = N_EXPERTS
    out8 = pl.BlockSpec((TOP_K, tm), lambda i: (0, i))
    return pl.pallas_call(
        functools.partial(_router_kernel, TM=tm), grid=(T // tm,),
        in_specs=[pl.BlockSpec((tm, D), lambda i: (i, 0)), pl.BlockSpec((E, D), lambda i: (0, 0)),
                  pl.BlockSpec((E, 1), lambda i: (0, 0))],
        out_specs=[out8, out8, out8, pl.BlockSpec((E, 1), lambda i: (0, 0))],
        out_shape=[jax.ShapeDtypeStruct((TOP_K, T), I32), jax.ShapeDtypeStruct((TOP_K, T), F32),
                   jax.ShapeDtypeStruct((TOP_K, T), I32), jax.ShapeDtypeStruct((E, 1), F32)],
        scratch_shapes=[pltpu.VMEM((E, 1), F32)],
        compiler_params=_cparams(), name="router")(h1, w_router.T, router_bias.reshape(E, 1))


def _row_copy(src, src_row, dst, dst_row, sem):
    return pltpu.make_async_copy(src.at[pl.ds(src_row, 1)], dst.at[pl.ds(dst_row, 1)], sem)


def _dispatch_kernel(dest_ref, h_ref, xs_hbm, sem, *, TM):
    def start(n, carry):
        for k in range(TOP_K):
            _row_copy(h_ref, n, xs_hbm, dest_ref[k, n], sem).start(priority=k % 2)
        return carry

    def wait(n, carry):
        for k in range(TOP_K):
            _row_copy(h_ref, 0, xs_hbm, 0, sem).wait()
        return carry

    lax.fori_loop(0, TM, start, 0)
    lax.fori_loop(0, TM, wait, 0)


def _dest_spec(tm):
    return pl.BlockSpec((TOP_K, tm), lambda i: (0, i), memory_space=pltpu.SMEM)


def dispatch(dest, h1, tm=ROW_TILE):
    T, D = h1.shape
    return pl.pallas_call(
        functools.partial(_dispatch_kernel, TM=tm), grid=(T // tm,),
        in_specs=[_dest_spec(tm), pl.BlockSpec((tm, D), lambda i: (i, 0))],
        out_specs=pl.BlockSpec(memory_space=pl.ANY),
        out_shape=jax.ShapeDtypeStruct((T * TOP_K, D), h1.dtype),
        scratch_shapes=[pltpu.SemaphoreType.DMA(())],
        compiler_params=_cparams(), name="moe_dispatch")(dest, h1)


VISIT_FIRST_OF_TILE = 1
VISIT_FIRST_OF_EXPERT = 2
VISIT_ACTIVE = 4
VISIT_HAS_NEXT_EXPERT = 8
X_RING = 3


def _expert_kernel(vt_ref, ve_ref, vlo_ref, vhi_ref, vfl_ref, ven_ref, vsl_ref,
                   xs_hbm, wg_hbm, wu_hbm, wd_hbm, o_ref,
                   xbuf_ref, wgf_ref, wuf_ref, wdf_ref, wgb_ref, wub_ref, wdb_ref, xsem, wsem, *, TE, NT):
    v = pl.program_id(0)
    flags = vfl_ref[v]
    slot = vsl_ref[v]
    tile = vt_ref[v]

    def row_tile_copy(t):
        s = lax.rem(t, X_RING)
        return pltpu.make_async_copy(xs_hbm.at[pl.ds(pl.multiple_of(t * TE, TE), TE)], xbuf_ref.at[s], xsem.at[s])

    def weight_copies(expert, s):
        return (pltpu.make_async_copy(wg_hbm.at[expert], wgf_ref.at[s], wsem.at[0, s]),
                pltpu.make_async_copy(wu_hbm.at[expert], wuf_ref.at[s], wsem.at[1, s]),
                pltpu.make_async_copy(wd_hbm.at[expert], wdf_ref.at[s], wsem.at[2, s]))

    @pl.when(v == 0)
    def _():
        for cp in weight_copies(ve_ref[0], 0):
            cp.start()
        for t in range(min(X_RING - 1, NT)):
            row_tile_copy(t).start()

    @pl.when((flags & VISIT_FIRST_OF_TILE) != 0)
    def _():
        row_tile_copy(tile).wait()

        @pl.when(tile + (X_RING - 1) < NT)
        def _():
            row_tile_copy(tile + (X_RING - 1)).start()

    @pl.when((flags & VISIT_FIRST_OF_EXPERT) != 0)
    def _():
        for cp in weight_copies(ve_ref[v], slot):
            cp.wait()

        @pl.when((flags & VISIT_HAS_NEXT_EXPERT) != 0)
        def _():
            for cp in weight_copies(ven_ref[v], 1 - slot):
                cp.start(priority=1)

        wgb_ref[...] = wgf_ref[slot].astype(BF16)
        wub_ref[...] = wuf_ref[slot].astype(BF16)
        wdb_ref[...] = wdf_ref[slot].astype(BF16)

    @pl.when((flags & VISIT_ACTIVE) != 0)
    def _():
        x = xbuf_ref[lax.rem(tile, X_RING)].astype(BF16)
        g = jnp.dot(x, wgb_ref[...], preferred_element_type=F32)
        u = jnp.dot(x, wub_ref[...], preferred_element_type=F32)
        rows = tile * TE + lax.broadcasted_iota(I32, (TE, 1), 0)
        mine = jnp.logical_and(rows >= vlo_ref[v], rows < vhi_ref[v])
        mid = jnp.where(mine, (g * _sigmoid(g)) * u, 0.0).astype(BF16)
        y = jnp.dot(mid, wdb_ref[...], preferred_element_type=F32)

        @pl.when((flags & VISIT_FIRST_OF_TILE) != 0)
        def _():
            o_ref[...] = y

        @pl.when((flags & VISIT_FIRST_OF_TILE) == 0)
        def _():
            o_ref[...] = o_ref[...] + y


def expert_ffn(tables, xs, w_gate_e, w_up_e, w_down_e, te=EXPERT_TILE):
    R, D = xs.shape
    F = EXPERT_FF
    nv = tables[0].shape[0]
    row_map = lambda v, vt, *_: (vt[v], 0)
    hbm = pl.BlockSpec(memory_space=pl.ANY)
    grid_spec = pltpu.PrefetchScalarGridSpec(
        num_scalar_prefetch=len(tables), grid=(nv,),
        in_specs=[hbm, hbm, hbm, hbm],
        out_specs=pl.BlockSpec((te, D), row_map),
        scratch_shapes=[pltpu.VMEM((X_RING, te, D), F32),
                        pltpu.VMEM((2, D, F), F32), pltpu.VMEM((2, D, F), F32), pltpu.VMEM((2, F, D), F32),
                        pltpu.VMEM((D, F), BF16), pltpu.VMEM((D, F), BF16), pltpu.VMEM((F, D), BF16),
                        pltpu.SemaphoreType.DMA((X_RING,)), pltpu.SemaphoreType.DMA((3, 2))])
    return pl.pallas_call(
        functools.partial(_expert_kernel, TE=te, NT=R // te), grid_spec=grid_spec,
        out_shape=jax.ShapeDtypeStruct((R, D), F32),
        compiler_params=_cparams(), name="expert_ffn")(*tables, xs, w_gate_e, w_up_e, w_down_e)


def visit_tables(counts, n_rows, te=EXPERT_TILE):
    E = counts.shape[0]
    n_tiles = n_rows // te
    nv = n_tiles + E
    ends = jnp.cumsum(counts)
    starts = ends - counts
    first_tile = starts // te
    last_tile = jnp.maximum(ends - 1, 0) // te
    nvis = jnp.where(counts > 0, last_tile - first_tile + 1, 0)
    vis_end = jnp.cumsum(nvis)
    vis_start = vis_end - nvis
    total = vis_end[-1]
    ids = jnp.arange(E, dtype=I32)[None, :]
    v = jnp.arange(nv, dtype=I32)
    vc = jnp.minimum(v, total - 1)
    e = jnp.sum((vis_end[None, :] <= vc[:, None]).astype(I32), axis=1)
    onehot = e[:, None] == ids
    pick = lambda table: jnp.sum(jnp.where(onehot, table[None, :], 0), axis=1)
    tile = (pick(first_tile - vis_start) + vc).astype(I32)
    lo = jnp.maximum(pick(starts), tile * te).astype(I32)
    hi = jnp.minimum(pick(ends), (tile + 1) * te).astype(I32)
    e_next = jnp.min(jnp.where(jnp.logical_and(ids > e[:, None], nvis[None, :] > 0), ids, E), axis=1)
    n_before = jnp.sum(jnp.where(jnp.logical_and(ids < e[:, None], nvis[None, :] > 0), 1, 0), axis=1)
    active = v < total
    first_of_tile = jnp.concatenate([jnp.ones((1,), bool), tile[1:] != tile[:-1]])
    first_of_expert = jnp.concatenate([jnp.ones((1,), bool), e[1:] != e[:-1]])
    flags = jnp.where(active,
                      VISIT_ACTIVE + jnp.where(first_of_tile, VISIT_FIRST_OF_TILE, 0)
                      + jnp.where(first_of_expert, VISIT_FIRST_OF_EXPERT, 0)
                      + jnp.where(e_next < E, VISIT_HAS_NEXT_EXPERT, 0), 0).astype(I32)
    slot = (n_before % 2).astype(I32)
    return tile, e.astype(I32), lo, hi, flags, jnp.minimum(e_next, E - 1).astype(I32), slot


def _combine_kernel(dest_ref, gate_ref, h_ref, wgs_ref, wus_ref, wds_ref, g_ref, b_ref,
                    ys_hbm, o_ref, buf_ref, sem, *, TM):
    def start(n, carry):
        for k in range(TOP_K):
            _row_copy(ys_hbm, dest_ref[k, n], buf_ref.at[k], n, sem).start(priority=k % 2)
        return carry

    def wait(n, carry):
        for k in range(TOP_K):
            _row_copy(ys_hbm, 0, buf_ref.at[k], 0, sem).wait()
        return carry

    lax.fori_loop(0, TM, start, 0)
    h = h_ref[...]
    hb = h.astype(BF16)
    g = jnp.dot(hb, wgs_ref[...], preferred_element_type=F32)
    u = jnp.dot(hb, wus_ref[...], preferred_element_type=F32)
    shared = jnp.dot(((g * _sigmoid(g)) * u).astype(BF16), wds_ref[...], preferred_element_type=F32)
    gate = gate_ref[...].T
    lax.fori_loop(0, TM, wait, 0)
    routed = jnp.zeros(h.shape, F32)
    for k in range(TOP_K):
        routed = routed + buf_ref[k] * gate[:, k:k + 1]
    y = ALPHA * h + (routed + shared)
    o_ref[...] = _layer_norm_rows(y, g_ref[...], b_ref[...])


def combine(dest, gate, h1, w_gate_s, w_up_s, w_down_s, g, b, ys, tm=ROW_TILE):
    T, D = h1.shape
    F = SHARED_FF
    row = pl.BlockSpec((tm, D), lambda i: (i, 0))
    vec = pl.BlockSpec((1, D), lambda i: (0, 0))
    return pl.pallas_call(
        functools.partial(_combine_kernel, TM=tm), grid=(T // tm,),
        in_specs=[_dest_spec(tm), pl.BlockSpec((TOP_K, tm), lambda i: (0, i)), row,
                  pl.BlockSpec((D, F), lambda i: (0, 0)), pl.BlockSpec((D, F), lambda i: (0, 0)),
                  pl.BlockSpec((F, D), lambda i: (0, 0)), vec, vec,
                  pl.BlockSpec(memory_space=pl.ANY)],
        out_specs=row, out_shape=jax.ShapeDtypeStruct((T, D), F32),
        scratch_shapes=[pltpu.VMEM((TOP_K, tm, D), F32), pltpu.SemaphoreType.DMA(())],
        compiler_params=_cparams(), name="moe_combine_ln2")(
            dest, gate, h1, w_gate_s.astype(BF16), w_up_s.astype(BF16), w_down_s.astype(BF16),
            g.reshape(1, D), b.reshape(1, D), ys)


def _mixer(h0f, h0b, w_in, kv_norm_g, w_uk, w_uv, conv_w, conv_b, w_rg_a, b_rg_a, w_rg_x, b_rg_x, rg_lambda,
           w_branch_a, w_branch_b, w_out, ln1_g, ln1_b):
    o_c = Q_DIM
    o_qi = o_c + KV_LATENT
    o_ki = o_qi + IDX_Q_DIM
    o_xr = o_ki + IDX_DIM + IDX_HEADS
    n_gate = 4 * LRU_WIDTH
    w_kw = jnp.pad(w_in[:, o_ki:o_xr], ((0, 0), (0, LANES - IDX_DIM - IDX_HEADS)))
    w_all = jnp.concatenate([w_in[:, o_xr:], w_in[:, :o_c], w_in[:, o_qi:o_ki], w_in[:, o_c:o_qi], w_kw],
                            axis=1).astype(BF16)
    c_qq = n_gate
    c_c = c_qq + Q_DIM + IDX_Q_DIM
    c_kw = c_c + KV_LATENT
    zz = matmul(h0b, w_all, 0, n_gate, F32, 1024, 1024, "proj_lru_gates")
    qq = matmul(h0b, w_all, c_qq, Q_DIM + IDX_Q_DIM, BF16, 1024, 1024, "proj_q")
    c, kw = proj_kv_idx(h0b, w_all, c_c, c_kw, kv_norm_g)
    attn = dsa_attention(qq, kw, c, w_uk, w_uv)
    lru = rg_lru(zz, conv_w, conv_b, w_rg_a, b_rg_a, w_rg_x, b_rg_x, rg_lambda)
    merged = merge_branches(attn, lru, w_branch_a.astype(BF16), w_branch_b.astype(BF16), zz)
    return out_proj_ln(merged, w_out.astype(BF16), h0f, ln1_g, ln1_b)


def _moe(h1, w_router, router_bias, w_gate_e, w_up_e, w_down_e, w_gate_s, w_up_s, w_down_s, ln2_g, ln2_b):
    T = h1.shape[0]
    eidx, gate, pos, cnt = router(h1, w_router, router_bias)
    counts = cnt[:, 0].astype(I32)
    starts = jnp.cumsum(counts) - counts
    ids = jnp.arange(N_EXPERTS, dtype=I32)[:, None, None]
    dest = jnp.sum(jnp.where(eidx[None] == ids, starts[:, None, None], 0), axis=0) + pos
    xs = dispatch(dest, h1)
    ys = expert_ffn(visit_tables(counts, T * TOP_K), xs, w_gate_e, w_up_e, w_down_e)
    return combine(dest, gate, h1, w_gate_s, w_up_s, w_down_s, ln2_g, ln2_b, ys)


def kernel(x, ln_in_g, ln_in_b, w_in, kv_norm_g, w_uk, w_uv, conv_w, conv_b, w_rg_a, b_rg_a, w_rg_x, b_rg_x,
           rg_lambda, w_branch_a, w_branch_b, w_out, ln1_g, ln1_b, w_router, router_bias, w_gate_e, w_up_e,
           w_down_e, w_gate_s, w_up_s, w_down_s, ln2_g, ln2_b):
    B, T, D = x.shape
    assert B == 1 and D == D_MODEL and w_in.shape[0] == DEPTH
    hf, hb = ln_in(x.reshape(T, D), ln_in_g, ln_in_b)
    for l in range(DEPTH):
        h1 = _mixer(hf, hb, w_in[l], kv_norm_g[l], w_uk[l], w_uv[l], conv_w[l], conv_b[l], w_rg_a[l], b_rg_a[l],
                    w_rg_x[l], b_rg_x[l], rg_lambda[l], w_branch_a[l], w_branch_b[l], w_out[l], ln1_g[l], ln1_b[l])
        hf = _moe(h1, w_router[l], router_bias[l], w_gate_e[l], w_up_e[l], w_down_e[l], w_gate_s[l], w_up_s[l],
                  w_down_s[l], ln2_g[l], ln2_b[l])
        hb = hf.astype(BF16)
    return hf.reshape(B, T, D)
```
